```python
import math
import jax, jax.numpy as jnp
from jax import lax
import numpy as np

D_MODEL = 1024
BATCH = 2
SEQ = 16384
DEPTH = 4
DEC_BATCH = 8
DEC_SEQ = 2048
PAST_LEN = 128

HEAD_DIM = 64
A_Q_HEADS = 8
A_KV_HEADS = 2
B_Q_HEADS = 8
B_KV_HEADS = 2
MIX_WIDTH = (A_Q_HEADS + B_Q_HEADS) * HEAD_DIM
A_Q_W = A_Q_HEADS * HEAD_DIM
A_KV_W = A_KV_HEADS * HEAD_DIM
B_Q_W = B_Q_HEADS * HEAD_DIM
B_KV_W = B_KV_HEADS * HEAD_DIM
QKV_WIDTH = A_Q_W + 2 * A_KV_W + B_Q_W + 2 * B_KV_W
QKV_SPLITS = (A_Q_W, A_Q_W + A_KV_W, A_Q_W + 2 * A_KV_W,
              A_Q_W + 2 * A_KV_W + B_Q_W, A_Q_W + 2 * A_KV_W + B_Q_W + B_KV_W)
D_FF = 2816
GRID_W = 64
Q_BLOCK = 128
WINDOW = 128
NUM_BUCKETS = 32
MAX_DISTANCE = 128
ROPE_THETA = 10000.0
EPS = 1e-6
N_MOD = 9

kernel_name = 'hymba_axial_window_macaron_encoder'


def rms_norm(x, g):
    xf = x.astype(jnp.float32)
    y = xf * lax.rsqrt(jnp.mean(xf * xf, axis=-1, keepdims=True) + EPS)
    return (y * g.astype(jnp.float32)).astype(x.dtype)


def axial_rope_tables(seq_len):
    rows = seq_len // GRID_W
    row = jnp.broadcast_to(jnp.arange(rows, dtype=jnp.float32)[:, None], (rows, GRID_W)).reshape(-1)
    col = jnp.broadcast_to(jnp.arange(GRID_W, dtype=jnp.float32)[None, :], (rows, GRID_W)).reshape(-1)
    axis_dim = HEAD_DIM // 2
    freqs = ROPE_THETA ** (-jnp.arange(0, axis_dim, 2, dtype=jnp.float32) / axis_dim)
    ang_r = row[:, None] * freqs[None, :]
    ang_c = col[:, None] * freqs[None, :]
    return (jnp.cos(ang_r), jnp.sin(ang_r), jnp.cos(ang_c), jnp.sin(ang_c))


def rotate_half(x, cos, sin):
    f = x.shape[-1] // 2
    cos = cos[:, None, :]
    sin = sin[:, None, :]
    x1, x2 = x[..., :f], x[..., f:]
    return jnp.concatenate([x1 * cos - x2 * sin, x2 * cos + x1 * sin], axis=-1)


def apply_axial_rope(x, tabs):
    cr, sr, cc, sc = tabs
    xf = x.astype(jnp.float32)
    half = HEAD_DIM // 2
    out = jnp.concatenate([rotate_half(xf[..., :half], cr, sr),
                           rotate_half(xf[..., half:], cc, sc)], axis=-1)
    return out.astype(x.dtype)


def t5_bucket(rel):
    half = NUM_BUCKETS // 2
    max_exact = half // 2
    base = jnp.where(rel > 0, half, 0)
    n = jnp.abs(rel)
    nf = jnp.maximum(n, 1).astype(jnp.float32)
    large = max_exact + (jnp.log(nf / max_exact) / math.log(MAX_DISTANCE / max_exact)
                         * (half - max_exact)).astype(jnp.int32)
    large = jnp.minimum(large, half - 1)
    return (base + jnp.where(n < max_exact, n, large)).astype(jnp.int32)


def band_tables(seq_len):
    nb = seq_len // Q_BLOCK
    qq = jnp.arange(Q_BLOCK)[:, None]
    kk = jnp.arange(Q_BLOCK + 2 * WINDOW)[None, :]
    rel = kk - WINDOW - qq
    in_window = jnp.abs(rel) <= WINDOW
    key_pos = jnp.arange(nb)[:, None, None] * Q_BLOCK - WINDOW + kk[None]
    in_range = (key_pos >= 0) & (key_pos < seq_len)
    return rel, in_window[None] & in_range


def global_gqa(q, k, v):
    batch, seq_len, n_heads, d = q.shape
    kvh = k.shape[2]
    g = n_heads // kvh
    nb = seq_len // Q_BLOCK
    scale = HEAD_DIM ** -0.5
    qb = q.reshape(batch, nb, Q_BLOCK, kvh, g, d).transpose(1, 0, 2, 3, 4, 5)

    def block(q_blk):
        s = jnp.einsum('bqhgd,bshd->bhgqs', q_blk, k).astype(jnp.float32) * scale
        p = jax.nn.softmax(s, axis=-1).astype(v.dtype)
        return jnp.einsum('bhgqs,bshd->bqhgd', p, v)

    ob = lax.map(block, qb)
    return ob.transpose(1, 0, 2, 3, 4, 5).reshape(batch, seq_len, n_heads * d)


def window_gqa(q, k, v, band_bias, band_mask, sinks):
    batch, seq_len, n_heads, d = q.shape
    kvh = k.shape[2]
    g = n_heads // kvh
    nb = seq_len // Q_BLOCK
    scale = HEAD_DIM ** -0.5
    qb = q.reshape(batch, nb, Q_BLOCK, kvh, g, d)

    def bands(t):
        tp = jnp.pad(t, ((0, 0), (WINDOW, WINDOW), (0, 0), (0, 0)))
        tp = tp.reshape(batch, nb + 2, Q_BLOCK, kvh, d)
        return jnp.concatenate([tp[:, :-2], tp[:, 1:-1], tp[:, 2:]], axis=2)

    kb, vb = bands(k), bands(v)
    s = jnp.einsum('bnqhgd,bnkhd->bnhgqk', qb, kb).astype(jnp.float32) * scale + band_bias[None, None]
    s = jnp.where(band_mask[None, :, None, None], s, -jnp.inf)
    sink = sinks.astype(jnp.float32).reshape(1, 1, kvh, g, 1, 1)
    m = jnp.maximum(jnp.max(s, axis=-1, keepdims=True), sink)
    p = jnp.exp(s - m)
    denom = jnp.sum(p, axis=-1, keepdims=True) + jnp.exp(sink - m)
    p = (p / denom).astype(v.dtype)
    o = jnp.einsum('bnhgqk,bnkhd->bnqhgd', p, vb)
    return o.reshape(batch, seq_len, n_heads * d)


def swiglu(h, w_in, w_out):
    a, b = jnp.split(h @ w_in, 2, axis=-1)
    return (jax.nn.silu(a) * b) @ w_out


def token_mixing(h, w_qkv, g_qk, sinks, w_o, rope_tabs, band_bias, band_mask):
    batch, seq_len, _ = h.shape
    qkv = h @ w_qkv
    qa, ka, va, qb, kb, vb = jnp.split(qkv, QKV_SPLITS, axis=-1)

    def heads(t, n):
        return t.reshape(batch, seq_len, n, HEAD_DIM)

    qa = apply_axial_rope(rms_norm(heads(qa, A_Q_HEADS), g_qk[0]), rope_tabs)
    ka = apply_axial_rope(rms_norm(heads(ka, A_KV_HEADS), g_qk[1]), rope_tabs)
    va = heads(va, A_KV_HEADS)
    qb = rms_norm(heads(qb, B_Q_HEADS), g_qk[2])
    kb = rms_norm(heads(kb, B_KV_HEADS), g_qk[3])
    vb = heads(vb, B_KV_HEADS)
    out_a = global_gqa(qa, ka, va)
    out_b = window_gqa(qb, kb, vb, band_bias, band_mask, sinks)
    return jnp.concatenate([out_a, out_b], axis=-1) @ w_o


def encoder_trunk(x, c, w_ada, b_ada, g_norm, ffn1_w_in, ffn1_w_out, w_qkv, g_qk,
                  attn_sinks, rel_bias, w_o, ffn2_w_in, ffn2_w_out):
    batch, seq_len, _ = x.shape
    rope_tabs = axial_rope_tables(seq_len)
    rel, band_mask = band_tables(seq_len)
    band_bias = rel_bias.astype(jnp.float32)[t5_bucket(rel)]
    band_bias = band_bias.transpose(2, 0, 1).reshape(
        B_KV_HEADS, B_Q_HEADS // B_KV_HEADS, Q_BLOCK, Q_BLOCK + 2 * WINDOW)
    c_act = jax.nn.silu(c)
    for l in range(DEPTH):
        mod = (c_act @ w_ada[l] + b_ada[l]).reshape(batch, N_MOD, 1, D_MODEL)
        sh1, sc1, gt1, sh2, sc2, gt2, sh3, sc3, gt3 = jnp.split(mod, N_MOD, axis=1)
        sh1, sc1, gt1, sh2, sc2, gt2, sh3, sc3, gt3 = (t[:, 0] for t in (sh1, sc1, gt1, sh2, sc2, gt2, sh3, sc3, gt3))
        h = rms_norm(x, g_norm[l, 0]) * (1 + sc1) + sh1
        x = x + 0.5 * gt1 * swiglu(h, ffn1_w_in[l], ffn1_w_out[l])
        h = rms_norm(x, g_norm[l, 1]) * (1 + sc2) + sh2
        x = x + gt2 * token_mixing(h, w_qkv[l], g_qk[l], attn_sinks[l], w_o[l],
                                   rope_tabs, band_bias, band_mask)
        h = rms_norm(x, g_norm[l, 2]) * (1 + sc3) + sh3
        x = x + 0.5 * gt3 * swiglu(h, ffn2_w_in[l], ffn2_w_out[l])
    return x


def setup_inputs(seed: int = 0) -> dict:
    key = jax.random.key(seed)
    ks = jax.random.split(key, 18)
    f32 = jnp.float32
    nrm = lambda k, shape: jax.random.normal(k, shape, dtype=f32)
    return {
        'x_prompt': nrm(ks[0], (BATCH, SEQ, D_MODEL)),
        'x_sample': nrm(ks[1], (DEC_BATCH, DEC_SEQ, D_MODEL)),
        'c_prompt': nrm(ks[2], (BATCH, D_MODEL)),
        'c_sample': nrm(ks[3], (DEC_BATCH, D_MODEL)),
        'w_ada': nrm(ks[4], (DEPTH, D_MODEL, N_MOD * D_MODEL)) * (0.5 * D_MODEL ** -0.5),
        'b_ada': nrm(ks[5], (DEPTH, N_MOD * D_MODEL)) * 0.01,
        'g_norm': 1.0 + 0.01 * nrm(ks[6], (DEPTH, 3, D_MODEL)),
        'ffn1_w_in': nrm(ks[7], (DEPTH, D_MODEL, 2 * D_FF)) * D_MODEL ** -0.5,
        'ffn1_w_out': nrm(ks[8], (DEPTH, D_FF, D_MODEL)) * D_FF ** -0.5,
        'w_qkv': nrm(ks[9], (DEPTH, D_MODEL, QKV_WIDTH)) * D_MODEL ** -0.5,
        'g_qk': 1.0 + 0.01 * nrm(ks[10], (DEPTH, 4, HEAD_DIM)),
        'attn_sinks': 0.5 * nrm(ks[11], (DEPTH, B_Q_HEADS)),
        'rel_bias': 0.1 * nrm(ks[12], (NUM_BUCKETS, B_Q_HEADS)),
        'w_o': nrm(ks[13], (DEPTH, MIX_WIDTH, D_MODEL)) * MIX_WIDTH ** -0.5,
        'ffn2_w_in': nrm(ks[14], (DEPTH, D_MODEL, 2 * D_FF)) * D_MODEL ** -0.5,
        'ffn2_w_out': nrm(ks[15], (DEPTH, D_FF, D_MODEL)) * D_FF ** -0.5,
    }


def reference(x_prompt, x_sample, c_prompt, c_sample, w_ada, b_ada, g_norm, ffn1_w_in, ffn1_w_out,
              w_qkv, g_qk, attn_sinks, rel_bias, w_o, ffn2_w_in, ffn2_w_out):
    y_prompt = encoder_trunk(x_prompt, c_prompt, w_ada, b_ada, g_norm, ffn1_w_in, ffn1_w_out, w_qkv,
                             g_qk, attn_sinks, rel_bias, w_o, ffn2_w_in, ffn2_w_out)
    y_sample = encoder_trunk(x_sample, c_sample, w_ada, b_ada, g_norm, ffn1_w_in, ffn1_w_out, w_qkv,
                             g_qk, attn_sinks, rel_bias, w_o, ffn2_w_in, ffn2_w_out)
    return (y_prompt, y_sample)
```

```python
import functools
import math

import jax
import jax.numpy as jnp
import numpy as np
from jax import lax
from jax.experimental import pallas as pl
from jax.experimental.pallas import tpu as pltpu

F32 = jnp.float32
BF16 = jnp.bfloat16

D_MODEL = 1024
DEPTH = 4
HEAD_DIM = 64
Q_HEADS = 8
KV_HEADS = 2
GROUP = Q_HEADS // KV_HEADS
Q_W = Q_HEADS * HEAD_DIM
KV_W = KV_HEADS * HEAD_DIM
QKV_W = 2 * (Q_W + 2 * KV_W)
D_FF = 2816
GRID_W = 64
Q_BLOCK = 128
WINDOW = 128
BAND = Q_BLOCK + 2 * WINDOW
NUM_BUCKETS = 32
MAX_DISTANCE = 128
ROPE_THETA = 10000.0
EPS = 1e-6
N_MOD = 9
SM_SCALE = HEAD_DIM ** -0.5

LANES = 128
MXU_DIM = 256
VMEM_LIMIT = 56 * 1024 * 1024

C_ROWS = 16


def _dot(a, b):
    return jnp.dot(a, b, preferred_element_type=F32)


def _dot_nt(a, b):
    return lax.dot_general(a, b, (((1,), (1,)), ((), ())), preferred_element_type=F32)


def _silu(a):
    return a / (1.0 + jnp.exp(-a))


def _params(*sem):
    return pltpu.CompilerParams(dimension_semantics=sem, vmem_limit_bytes=VMEM_LIMIT)


def _mod_kernel(c_ref, w_ref, b_ref, o_ref):
    c = c_ref[...]
    o_ref[0] = _dot(_silu(c).astype(BF16), w_ref[0].astype(BF16)) + b_ref[0]


def _modulation(c_all, w_ada, b_ada):
    tn = D_MODEL
    n_out = N_MOD * D_MODEL
    return pl.pallas_call(
        _mod_kernel,
        grid=(DEPTH, n_out // tn),
        in_specs=[
            pl.BlockSpec((C_ROWS, D_MODEL), lambda l, j: (0, 0)),
            pl.BlockSpec((1, D_MODEL, tn), lambda l, j: (l, 0, j)),
            pl.BlockSpec((1, 1, tn), lambda l, j: (l, 0, j)),
        ],
        out_specs=pl.BlockSpec((1, C_ROWS, tn), lambda l, j: (l, 0, j)),
        out_shape=jax.ShapeDtypeStruct((DEPTH, C_ROWS, n_out), F32),
        compiler_params=_params("parallel", "parallel"),
        name="adaln_mod",
    )(c_all, w_ada, b_ada.reshape(DEPTH, 1, n_out))


def _bias_kernel(rb_ref, bucket_ref, o_ref):
    bucket = bucket_ref[...]
    for h in range(Q_HEADS):
        acc = jnp.zeros(bucket.shape, F32)
        for b in range(NUM_BUCKETS):
            acc = jnp.where(bucket == b, rb_ref[b, h], acc)
        o_ref[h] = acc


def _t5_bucket(rel):
    half = NUM_BUCKETS // 2
    max_exact = half // 2
    base = jnp.where(rel > 0, half, 0)
    n = jnp.abs(rel)
    nf = jnp.maximum(n, 1).astype(F32)
    large = max_exact + (jnp.log(nf / max_exact) / math.log(MAX_DISTANCE / max_exact)
                         * (half - max_exact)).astype(jnp.int32)
    large = jnp.minimum(large, half - 1)
    return (base + jnp.where(n < max_exact, n, large)).astype(jnp.int32)


def _band_bias(rel_bias):
    qq = jnp.arange(Q_BLOCK)[:, None]
    kk = jnp.arange(BAND)[None, :]
    bucket = _t5_bucket(kk - WINDOW - qq)
    return pl.pallas_call(
        _bias_kernel,
        in_specs=[
            pl.BlockSpec(memory_space=pltpu.SMEM),
            pl.BlockSpec((Q_BLOCK, BAND), lambda: (0, 0)),
        ],
        out_specs=pl.BlockSpec((Q_HEADS, Q_BLOCK, BAND), lambda: (0, 0, 0)),
        out_shape=jax.ShapeDtypeStruct((Q_HEADS, Q_BLOCK, BAND), F32),
        name="band_bias",
    )(rel_bias.astype(F32), bucket)


def _ada_norm(x, g, scale, shift):
    y = x * lax.rsqrt(jnp.mean(x * x, axis=-1, keepdims=True) + EPS)
    return (y * g) * (1.0 + scale) + shift


def _ffn_kernel(x_ref, mod_ref, g_ref, wa_ref, wb_ref, wo_ref, o_ref, h_ref, acc_ref, *, rows, n_ff):
    j = pl.program_id(2)
    sh_i, sc_i, gt_i = rows

    @pl.when(j == 0)
    def _():
        h = _ada_norm(x_ref[0], g_ref[0], mod_ref[0, sc_i:sc_i + 1, :], mod_ref[0, sh_i:sh_i + 1, :])
        h_ref[...] = h.astype(BF16)
        acc_ref[...] = jnp.zeros(acc_ref.shape, F32)

    h = h_ref[...]
    a = _dot(h, wa_ref[0])
    b = _dot(h, wb_ref[0])
    acc_ref[...] += _dot((_silu(a) * b).astype(BF16), wo_ref[0])

    @pl.when(j == n_ff - 1)
    def _():
        o_ref[0] = x_ref[0] + 0.5 * mod_ref[0, gt_i:gt_i + 1, :] * acc_ref[...]


def _ffn(x, mod, g_norm, w_in, w_out, layer, sub, tm, tf):
    batch, seq, _ = x.shape
    n_ff = D_FF // tf
    rows = (3 * sub, 3 * sub + 1, 3 * sub + 2)
    return pl.pallas_call(
        functools.partial(_ffn_kernel, rows=rows, n_ff=n_ff),
        grid=(batch, seq // tm, n_ff),
        in_specs=[
            pl.BlockSpec((1, tm, D_MODEL), lambda b, i, j: (b, i, 0)),
            pl.BlockSpec((1, N_MOD, D_MODEL), lambda b, i, j: (b, 0, 0)),
            pl.BlockSpec((1, 1, D_MODEL), lambda b, i, j: (3 * layer + sub, 0, 0)),
            pl.BlockSpec((1, D_MODEL, tf), lambda b, i, j: (layer, 0, j)),
            pl.BlockSpec((1, D_MODEL, tf), lambda b, i, j: (layer, 0, j + n_ff)),
            pl.BlockSpec((1, tf, D_MODEL), lambda b, i, j: (layer, j, 0)),
        ],
        out_specs=pl.BlockSpec((1, tm, D_MODEL), lambda b, i, j: (b, i, 0)),
        out_shape=jax.ShapeDtypeStruct(x.shape, F32),
        scratch_shapes=[pltpu.VMEM((tm, D_MODEL), BF16), pltpu.VMEM((tm, D_MODEL), F32)],
        compiler_params=_params("parallel", "parallel", "arbitrary"),
        name="ffn",
    )(x, mod, g_norm, w_in, w_in, w_out)


def _head_sumsq(t, ones_bd):
    sq = t * t
    hi = sq.astype(BF16)
    lo = (sq - hi.astype(F32)).astype(BF16)
    return _dot(hi, ones_bd) + _dot(lo, ones_bd)


def _rope(t, cos, sin, first_half):
    up = pltpu.roll(t, LANES - HEAD_DIM // 4, axis=1)
    dn = pltpu.roll(t, HEAD_DIM // 4, axis=1)
    return t * cos + jnp.where(first_half, up, dn) * sin


def _qkv_kernel(x_ref, mod_ref, g_ref, w_ref, gain_ref, cos_ref, sin_ref, ones_ref,
                qta_ref, ka_ref, vxa_ref, qb_ref, kvb_ref):
    tm = x_ref.shape[1]
    h = _ada_norm(x_ref[0], g_ref[0], mod_ref[0, 4:5, :], mod_ref[0, 3:4, :]).astype(BF16)
    qkv = _dot(h, w_ref[0])
    ones_bd = ones_ref[...]
    gain = gain_ref[0]

    normed = []
    for s in range(QKV_W // MXU_DIM):
        t = qkv[:, s * MXU_DIM:(s + 1) * MXU_DIM]
        ss = _head_sumsq(t, ones_bd)
        normed.append(t * lax.rsqrt(ss * (1.0 / HEAD_DIM) + EPS)
                      * gain[:, s * MXU_DIM:(s + 1) * MXU_DIM])
    normed = jnp.concatenate(normed, axis=1)

    cos = cos_ref[...]
    sin = sin_ref[...]
    lane = lax.broadcasted_iota(jnp.int32, (tm, LANES), 1)
    first_half = (lane % (HEAD_DIM // 2)) < (HEAD_DIM // 4)

    qa = [_rope(normed[:, s * LANES:(s + 1) * LANES], cos, sin, first_half) * SM_SCALE
          for s in range(Q_W // LANES)]
    qa = jnp.concatenate(qa, axis=1)
    qta_ref[0] = qa.T.astype(BF16)
    ka_ref[0] = _rope(normed[:, Q_W:Q_W + KV_W], cos, sin, first_half).astype(BF16)
    va_t = qkv[:, Q_W + KV_W:Q_W + 2 * KV_W].T
    ones = jnp.ones((HEAD_DIM, tm), F32)
    for kv in range(KV_HEADS):
        vxa_ref[0, kv] = jnp.concatenate(
            [va_t[kv * HEAD_DIM:(kv + 1) * HEAD_DIM], ones], axis=0).astype(BF16)

    off = Q_W + 2 * KV_W
    qb_ref[0] = (normed[:, off:off + Q_W] * SM_SCALE).astype(BF16)
    kb = normed[:, off + Q_W:off + Q_W + KV_W]
    vb = qkv[:, off + Q_W + KV_W:off + Q_W + 2 * KV_W]
    kvb_ref[0] = jnp.concatenate(
        [kb, pltpu.roll(kb, HEAD_DIM, axis=1), vb, pltpu.roll(vb, HEAD_DIM, axis=1)],
        axis=1).astype(BF16)


def _qkv(x, mod, g_norm, w_qkv, gains, cos, sin, ones_bd, layer, tm):
    batch, seq, _ = x.shape
    return pl.pallas_call(
        _qkv_kernel,
        grid=(batch, seq // tm),
        in_specs=[
            pl.BlockSpec((1, tm, D_MODEL), lambda b, i: (b, i, 0)),
            pl.BlockSpec((1, N_MOD, D_MODEL), lambda b, i: (b, 0, 0)),
            pl.BlockSpec((1, 1, D_MODEL), lambda b, i: (3 * layer + 1, 0, 0)),
            pl.BlockSpec((1, D_MODEL, QKV_W), lambda b, i: (layer, 0, 0)),
            pl.BlockSpec((1, 1, QKV_W), lambda b, i: (layer, 0, 0)),
            pl.BlockSpec((tm, LANES), lambda b, i: (i, 0)),
            pl.BlockSpec((tm, LANES), lambda b, i: (i, 0)),
            pl.BlockSpec((MXU_DIM, MXU_DIM), lambda b, i: (0, 0)),
        ],
        out_specs=[
            pl.BlockSpec((1, Q_W, tm), lambda b, i: (b, 0, i)),
            pl.BlockSpec((1, tm, KV_W), lambda b, i: (b, i, 0)),
            pl.BlockSpec((1, KV_HEADS, 2 * HEAD_DIM, tm), lambda b, i: (b, 0, 0, i)),
            pl.BlockSpec((1, tm, Q_W), lambda b, i: (b, i, 0)),
            pl.BlockSpec((1, tm, 4 * KV_W), lambda b, i: (b, i, 0)),
        ],
        out_shape=[
            jax.ShapeDtypeStruct((batch, Q_W, seq), BF16),
            jax.ShapeDtypeStruct((batch, seq, KV_W), BF16),
            jax.ShapeDtypeStruct((batch, KV_HEADS, 2 * HEAD_DIM, seq), BF16),
            jax.ShapeDtypeStruct((batch, seq, Q_W), BF16),
            jax.ShapeDtypeStruct((batch, seq, 4 * KV_W), BF16),
        ],
        compiler_params=_params("parallel", "parallel"),
        name="qkv",
    )(x, mod, g_norm, w_qkv, gains, cos, sin, ones_bd)


def _attn_a_kernel(qt_ref, k_ref, vx_ref, o_ref, *, seq, tk):
    kv = pl.program_id(1)
    tq = qt_ref.shape[2]
    q4 = qt_ref[0]
    qt = jnp.concatenate([q4[g * HEAD_DIM:(g + 1) * HEAD_DIM] for g in range(GROUP)], axis=1)
    zero = jnp.zeros_like(qt)
    q_pad = jnp.concatenate([jnp.where(kv == h, qt, zero) for h in range(KV_HEADS)], axis=0)

    def body(j, carry):
        m, acc = carry
        off = pl.multiple_of(j * tk, tk)
        s_t = _dot(k_ref[0, pl.ds(off, tk), :], q_pad)
        m_new = jnp.maximum(m, jnp.max(s_t, axis=0, keepdims=True))
        alpha = jnp.exp(m - m_new)
        p_t = jnp.exp(s_t - m_new).astype(BF16)
        acc = alpha * acc + _dot(vx_ref[0, 0, :, pl.ds(off, tk)], p_t)
        return m_new, acc

    m0 = jnp.full((1, GROUP * tq), -jnp.inf, F32)
    acc0 = jnp.zeros((2 * HEAD_DIM, GROUP * tq), F32)
    _, acc = lax.fori_loop(0, seq // tk, body, (m0, acc0))
    out_t = acc[:HEAD_DIM] / acc[HEAD_DIM:HEAD_DIM + 1]
    out_t = jnp.concatenate([out_t[:, g * tq:(g + 1) * tq] for g in range(GROUP)], axis=0)
    o_ref[0] = out_t.T.astype(BF16)


def _attn_a(qt_a, k_a, vx_a, tq, tk):
    batch, _, seq = qt_a.shape
    gw = GROUP * HEAD_DIM
    return pl.pallas_call(
        functools.partial(_attn_a_kernel, seq=seq, tk=tk),
        grid=(batch, KV_HEADS, seq // tq),
        in_specs=[
            pl.BlockSpec((1, gw, tq), lambda b, h, i: (b, h, i)),
            pl.BlockSpec((1, seq, KV_W), lambda b, h, i: (b, 0, 0)),
            pl.BlockSpec((1, 1, 2 * HEAD_DIM, seq), lambda b, h, i: (b, h, 0, 0)),
        ],
        out_specs=pl.BlockSpec((1, tq, gw), lambda b, h, i: (b, i, h)),
        out_shape=jax.ShapeDtypeStruct((batch, seq, Q_W), BF16),
        compiler_params=_params("parallel", "parallel", "parallel"),
        name="attn_global",
    )(qt_a, k_a, vx_a)


def _attn_b_kernel(sink_ref, q_ref, kp_ref, kc_ref, kn_ref, bias_ref, o_ref, *, seq, layer):
    n = pl.program_id(1)
    q = q_ref[0]
    band = jnp.concatenate([kp_ref[0], kc_ref[0], kn_ref[0]], axis=0)
    k_same, k_swap = band[:, 0:KV_W], band[:, KV_W:2 * KV_W]
    v_same, v_swap = band[:, 2 * KV_W:3 * KV_W], band[:, 3 * KV_W:4 * KV_W]

    qq = lax.broadcasted_iota(jnp.int32, (Q_BLOCK, BAND), 0)
    kk = lax.broadcasted_iota(jnp.int32, (Q_BLOCK, BAND), 1)
    key_pos = n * Q_BLOCK - WINDOW + kk
    rel = kk - WINDOW - qq
    visible = (rel >= -WINDOW) & (rel <= WINDOW) & (key_pos >= 0) & (key_pos < seq)
    lane = lax.broadcasted_iota(jnp.int32, (Q_BLOCK, LANES), 1)
    low = lane < HEAD_DIM
    zero = jnp.zeros((Q_BLOCK, LANES), BF16)

    for pair in range(Q_HEADS // 2):
        kv = (2 * pair) // GROUP
        q_pair = q[:, pair * LANES:(pair + 1) * LANES]
        halves = []
        for par in range(2):
            h = 2 * pair + par
            same = (kv == par)
            q_h = jnp.where(low, q_pair, zero) if par == 0 else jnp.where(low, zero, q_pair)
            s = _dot_nt(q_h, k_same if same else k_swap) + bias_ref[h]
            s = jnp.where(visible, s, -jnp.inf)
            sink = sink_ref[layer, h]
            m = jnp.maximum(jnp.max(s, axis=-1, keepdims=True), sink)
            p = jnp.exp(s - m)
            denom = jnp.sum(p, axis=-1, keepdims=True) + jnp.exp(sink - m)
            p = (p / denom).astype(BF16)
            halves.append(_dot(p, v_same if same else v_swap))
        o_ref[0, :, pair * LANES:(pair + 1) * LANES] = jnp.where(low, halves[0], halves[1]).astype(BF16)


def _attn_b(q_b, kv_b, band_bias, sinks, layer):
    batch, seq, _ = q_b.shape
    nb = seq // Q_BLOCK
    kv_spec = lambda f: pl.BlockSpec((1, Q_BLOCK, 4 * KV_W), f)
    return pl.pallas_call(
        functools.partial(_attn_b_kernel, seq=seq, layer=layer),
        grid=(batch, nb),
        in_specs=[
            pl.BlockSpec(memory_space=pltpu.SMEM),
            pl.BlockSpec((1, Q_BLOCK, Q_W), lambda b, n: (b, n, 0)),
            kv_spec(lambda b, n: (b, jnp.maximum(n - 1, 0), 0)),
            kv_spec(lambda b, n: (b, n, 0)),
            kv_spec(lambda b, n: (b, jnp.minimum(n + 1, nb - 1), 0)),
            pl.BlockSpec((Q_HEADS, Q_BLOCK, BAND), lambda b, n: (0, 0, 0)),
        ],
        out_specs=pl.BlockSpec((1, Q_BLOCK, Q_W), lambda b, n: (b, n, 0)),
        out_shape=jax.ShapeDtypeStruct((batch, seq, Q_W), BF16),
        compiler_params=_params("parallel", "parallel"),
        name="attn_window",
    )(sinks, q_b, kv_b, kv_b, kv_b, band_bias)


def _out_proj_kernel(x_ref, mod_ref, oa_ref, ob_ref, w_ref, o_ref):
    mix = jnp.concatenate([oa_ref[0], ob_ref[0]], axis=1)
    o_ref[0] = x_ref[0] + mod_ref[0, 5:6, :] * _dot(mix, w_ref[0])


def _out_proj(x, mod, out_a, out_b, w_o, layer, tm):
    batch, seq, _ = x.shape
    return pl.pallas_call(
        _out_proj_kernel,
        grid=(batch, seq // tm),
        in_specs=[
            pl.BlockSpec((1, tm, D_MODEL), lambda b, i: (b, i, 0)),
            pl.BlockSpec((1, N_MOD, D_MODEL), lambda b, i: (b, 0, 0)),
            pl.BlockSpec((1, tm, Q_W), lambda b, i: (b, i, 0)),
            pl.BlockSpec((1, tm, Q_W), lambda b, i: (b, i, 0)),
            pl.BlockSpec((1, 2 * Q_W, D_MODEL), lambda b, i: (layer, 0, 0)),
        ],
        out_specs=pl.BlockSpec((1, tm, D_MODEL), lambda b, i: (b, i, 0)),
        out_shape=jax.ShapeDtypeStruct(x.shape, F32),
        compiler_params=_params("parallel", "parallel"),
        name="out_proj",
    )(x, mod, out_a, out_b, w_o)


def _rope_tables(seq):
    pos = jnp.arange(seq)
    row = (pos // GRID_W).astype(F32)
    col = (pos % GRID_W).astype(F32)
    axis_dim = HEAD_DIM // 2
    freqs = ROPE_THETA ** (-jnp.arange(0, axis_dim, 2, dtype=F32) / axis_dim)
    ang_r = row[:, None] * freqs[None, :]
    ang_c = col[:, None] * freqs[None, :]
    cr, sr, cc, sc = jnp.cos(ang_r), jnp.sin(ang_r), jnp.cos(ang_c), jnp.sin(ang_c)
    cos = jnp.concatenate([cr, cr, cc, cc], axis=1)
    sin = jnp.concatenate([-sr, sr, -sc, sc], axis=1)
    reps = LANES // HEAD_DIM
    return jnp.tile(cos, (1, reps)), jnp.tile(sin, (1, reps))


def _qkv_gains(g_qk):
    ones = jnp.ones((DEPTH, KV_W), F32)
    t = lambda i, n: jnp.tile(g_qk[:, i, :].astype(F32), (1, n))
    return jnp.concatenate([t(0, Q_HEADS), t(1, KV_HEADS), ones,
                            t(2, Q_HEADS), t(3, KV_HEADS), ones], axis=1)[:, None, :]


def _tiles(seq):
    tm_ffn = min(1024, seq)
    tm_qkv = min(512, seq)
    tk = min(512, seq)
    return tm_ffn, tm_qkv, tk


def _trunk(x, mod_all, weights, band_bias):
    seq = x.shape[1]
    tm_ffn, tm_qkv, tk = _tiles(seq)
    cos, sin = _rope_tables(seq)
    for l in range(DEPTH):
        mod = mod_all[l]
        x = _ffn(x, mod, weights["g_norm"], weights["ffn1_w_in"], weights["ffn1_w_out"], l, 0, tm_ffn, MXU_DIM)
        qt_a, k_a, vx_a, q_b, kv_b = _qkv(x, mod, weights["g_norm"], weights["w_qkv"], weights["gains"],
                                         cos, sin, weights["ones_bd"], l, tm_qkv)
        out_a = _attn_a(qt_a, k_a, vx_a, Q_BLOCK, tk)
        out_b = _attn_b(q_b, kv_b, band_bias, weights["sinks"], l)
        x = _out_proj(x, mod, out_a, out_b, weights["w_o"], l, tm_qkv)
        x = _ffn(x, mod, weights["g_norm"], weights["ffn2_w_in"], weights["ffn2_w_out"], l, 2, tm_ffn, MXU_DIM)
    return x


def _prepare(w_ada, b_ada, g_norm, ffn1_w_in, ffn1_w_out, w_qkv, g_qk, attn_sinks, w_o,
             ffn2_w_in, ffn2_w_out):
    head_id = np.arange(MXU_DIM) // HEAD_DIM
    return {
        "g_norm": g_norm.astype(F32).reshape(DEPTH * 3, 1, D_MODEL),
        "ffn1_w_in": ffn1_w_in.astype(BF16), "ffn1_w_out": ffn1_w_out.astype(BF16),
        "ffn2_w_in": ffn2_w_in.astype(BF16), "ffn2_w_out": ffn2_w_out.astype(BF16),
        "w_qkv": w_qkv.astype(BF16), "w_o": w_o.astype(BF16),
        "gains": _qkv_gains(g_qk),
        "sinks": attn_sinks.astype(F32),
        "ones_bd": jnp.asarray(head_id[:, None] == head_id[None, :], BF16),
    }


def kernel(x_prompt, x_sample, c_prompt, c_sample, w_ada, b_ada, g_norm, ffn1_w_in, ffn1_w_out,
           w_qkv, g_qk, attn_sinks, rel_bias, w_o, ffn2_w_in, ffn2_w_out):
    weights = _prepare(w_ada, b_ada, g_norm, ffn1_w_in, ffn1_w_out, w_qkv, g_qk, attn_sinks, w_o,
                       ffn2_w_in, ffn2_w_out)
    nb_p, nb_s = c_prompt.shape[0], c_sample.shape[0]
    c_all = jnp.concatenate(
        [c_prompt, c_sample, jnp.zeros((C_ROWS - nb_p - nb_s, D_MODEL), F32)], axis=0)
    mod = _modulation(c_all, w_ada, b_ada).reshape(DEPTH, C_ROWS, N_MOD, D_MODEL)
    band_bias = _band_bias(rel_bias)
    y_prompt = _trunk(x_prompt, mod[:, :nb_p], weights, band_bias)
    y_sample = _trunk(x_sample, mod[:, nb_p:nb_p + nb_s], weights, band_bias)
    return (y_prompt, y_sample)
```

```python
import functools
import math

import jax
import jax.numpy as jnp
import numpy as np
from jax import lax
from jax.experimental import pallas as pl
from jax.experimental.pallas import tpu as pltpu

F32 = jnp.float32
BF16 = jnp.bfloat16

D_MODEL = 1024
DEPTH = 4
HEAD_DIM = 64
Q_HEADS = 8
KV_HEADS = 2
GROUP = Q_HEADS // KV_HEADS
Q_W = Q_HEADS * HEAD_DIM
KV_W = KV_HEADS * HEAD_DIM
QKV_W = 2 * (Q_W + 2 * KV_W)
D_FF = 2816
GRID_W = 64
Q_BLOCK = 128
WINDOW = 128
BAND = Q_BLOCK + 2 * WINDOW
NUM_BUCKETS = 32
MAX_DISTANCE = 128
ROPE_THETA = 10000.0
EPS = 1e-6
N_MOD = 9
SM_SCALE = HEAD_DIM ** -0.5
LOG2E = math.log2(math.e)
BF16_SUBLANES = 16
VX_ROWS = HEAD_DIM + BF16_SUBLANES

LANES = 128
MXU_DIM = 256
VMEM_LIMIT = 56 * 1024 * 1024

C_ROWS = 16


def _dot(a, b):
    return jnp.dot(a, b, preferred_element_type=F32)


def _dot_nt(a, b):
    return lax.dot_general(a, b, (((1,), (1,)), ((), ())), preferred_element_type=F32)


def _silu(a):
    return a / (1.0 + jnp.exp(-a))


def _params(*sem):
    return pltpu.CompilerParams(dimension_semantics=sem, vmem_limit_bytes=VMEM_LIMIT)


def _mod_kernel(c_ref, w_ref, b_ref, o_ref):
    c = c_ref[...]
    o_ref[0] = _dot(_silu(c).astype(BF16), w_ref[0].astype(BF16)) + b_ref[0]


def _modulation(c_all, w_ada, b_ada):
    tn = D_MODEL
    n_out = N_MOD * D_MODEL
    return pl.pallas_call(
        _mod_kernel,
        grid=(DEPTH, n_out // tn),
        in_specs=[
            pl.BlockSpec((C_ROWS, D_MODEL), lambda l, j: (0, 0)),
            pl.BlockSpec((1, D_MODEL, tn), lambda l, j: (l, 0, j)),
            pl.BlockSpec((1, 1, tn), lambda l, j: (l, 0, j)),
        ],
        out_specs=pl.BlockSpec((1, C_ROWS, tn), lambda l, j: (l, 0, j)),
        out_shape=jax.ShapeDtypeStruct((DEPTH, C_ROWS, n_out), F32),
        compiler_params=_params("parallel", "parallel"),
        name="adaln_mod",
    )(c_all, w_ada, b_ada.reshape(DEPTH, 1, n_out))


def _bias_kernel(rb_ref, bucket_ref, o_ref):
    bucket = bucket_ref[...]
    for h in range(Q_HEADS):
        acc = jnp.zeros(bucket.shape, F32)
        for b in range(NUM_BUCKETS):
            acc = jnp.where(bucket == b, rb_ref[b, h], acc)
        o_ref[h] = acc


def _t5_bucket(rel):
    half = NUM_BUCKETS // 2
    max_exact = half // 2
    base = jnp.where(rel > 0, half, 0)
    n = jnp.abs(rel)
    nf = jnp.maximum(n, 1).astype(F32)
    large = max_exact + (jnp.log(nf / max_exact) / math.log(MAX_DISTANCE / max_exact)
                         * (half - max_exact)).astype(jnp.int32)
    large = jnp.minimum(large, half - 1)
    return (base + jnp.where(n < max_exact, n, large)).astype(jnp.int32)


def _band_bias(rel_bias):
    qq = jnp.arange(Q_BLOCK)[:, None]
    kk = jnp.arange(BAND)[None, :]
    bucket = _t5_bucket(kk - WINDOW - qq)
    return pl.pallas_call(
        _bias_kernel,
        in_specs=[
            pl.BlockSpec(memory_space=pltpu.SMEM),
            pl.BlockSpec((Q_BLOCK, BAND), lambda: (0, 0)),
        ],
        out_specs=pl.BlockSpec((Q_HEADS, Q_BLOCK, BAND), lambda: (0, 0, 0)),
        out_shape=jax.ShapeDtypeStruct((Q_HEADS, Q_BLOCK, BAND), F32),
        name="band_bias",
    )(rel_bias.astype(F32), bucket)


def _ada_norm(x, g, scale, shift):
    y = x * lax.rsqrt(jnp.mean(x * x, axis=-1, keepdims=True) + EPS)
    return (y * g) * (1.0 + scale) + shift


def _ffn_kernel(x_ref, mod_ref, g_ref, wa_ref, wb_ref, wo_ref, o_ref, h_ref, acc_ref, *, rows, n_ff):
    j = pl.program_id(2)
    sh_i, sc_i, gt_i = rows

    @pl.when(j == 0)
    def _():
        h = _ada_norm(x_ref[0], g_ref[0], mod_ref[0, sc_i:sc_i + 1, :], mod_ref[0, sh_i:sh_i + 1, :])
        h_ref[...] = h.astype(BF16)
        acc_ref[...] = jnp.zeros(acc_ref.shape, F32)

    h = h_ref[...]
    a = _dot(h, wa_ref[0])
    b = _dot(h, wb_ref[0])
    acc_ref[...] += _dot((_silu(a) * b).astype(BF16), wo_ref[0])

    @pl.when(j == n_ff - 1)
    def _():
        o_ref[0] = x_ref[0] + 0.5 * mod_ref[0, gt_i:gt_i + 1, :] * acc_ref[...]


def _ffn(x, mod, g_norm, w_in, w_out, layer, sub, tm, tf):
    batch, seq, _ = x.shape
    n_ff = D_FF // tf
    rows = (3 * sub, 3 * sub + 1, 3 * sub + 2)
    return pl.pallas_call(
        functools.partial(_ffn_kernel, rows=rows, n_ff=n_ff),
        grid=(batch, seq // tm, n_ff),
        in_specs=[
            pl.BlockSpec((1, tm, D_MODEL), lambda b, i, j: (b, i, 0)),
            pl.BlockSpec((1, N_MOD, D_MODEL), lambda b, i, j: (b, 0, 0)),
            pl.BlockSpec((1, 1, D_MODEL), lambda b, i, j: (3 * layer + sub, 0, 0)),
            pl.BlockSpec((1, D_MODEL, tf), lambda b, i, j: (layer, 0, j)),
            pl.BlockSpec((1, D_MODEL, tf), lambda b, i, j: (layer, 0, j + n_ff)),
            pl.BlockSpec((1, tf, D_MODEL), lambda b, i, j: (layer, j, 0)),
        ],
        out_specs=pl.BlockSpec((1, tm, D_MODEL), lambda b, i, j: (b, i, 0)),
        out_shape=jax.ShapeDtypeStruct(x.shape, F32),
        scratch_shapes=[pltpu.VMEM((tm, D_MODEL), BF16), pltpu.VMEM((tm, D_MODEL), F32)],
        compiler_params=_params("parallel", "parallel", "arbitrary"),
        name="ffn",
    )(x, mod, g_norm, w_in, w_in, w_out)


def _head_sumsq(t, ones_bd):
    sq = t * t
    hi = sq.astype(BF16)
    lo = (sq - hi.astype(F32)).astype(BF16)
    return _dot(hi, ones_bd) + _dot(lo, ones_bd)


def _rope(t, cos, sin, first_half):
    up = pltpu.roll(t, LANES - HEAD_DIM // 4, axis=1)
    dn = pltpu.roll(t, HEAD_DIM // 4, axis=1)
    return t * cos + jnp.where(first_half, up, dn) * sin


def _qkv_kernel(x_ref, mod_ref, g_ref, w_ref, gain_ref, cos_ref, sin_ref, ones_ref,
                qta_ref, ka_ref, vxa_ref, qb_ref, kvb_ref):
    tm = x_ref.shape[1]
    h = _ada_norm(x_ref[0], g_ref[0], mod_ref[0, 4:5, :], mod_ref[0, 3:4, :]).astype(BF16)
    qkv = _dot(h, w_ref[0])
    ones_bd = ones_ref[...]
    gain = gain_ref[0]

    normed = []
    for s in range(QKV_W // MXU_DIM):
        t = qkv[:, s * MXU_DIM:(s + 1) * MXU_DIM]
        ss = _head_sumsq(t, ones_bd)
        normed.append(t * lax.rsqrt(ss * (1.0 / HEAD_DIM) + EPS)
                      * gain[:, s * MXU_DIM:(s + 1) * MXU_DIM])
    normed = jnp.concatenate(normed, axis=1)

    cos = cos_ref[...]
    sin = sin_ref[...]
    lane = lax.broadcasted_iota(jnp.int32, (tm, LANES), 1)
    first_half = (lane % (HEAD_DIM // 2)) < (HEAD_DIM // 4)

    qa = [_rope(normed[:, s * LANES:(s + 1) * LANES], cos, sin, first_half) * (SM_SCALE * LOG2E)
          for s in range(Q_W // LANES)]
    qa = jnp.concatenate(qa, axis=1)
    qta_ref[0] = qa.T.astype(BF16)
    ka_ref[0] = _rope(normed[:, Q_W:Q_W + KV_W], cos, sin, first_half).astype(BF16)
    va_t = qkv[:, Q_W + KV_W:Q_W + 2 * KV_W].T
    ones = jnp.ones((VX_ROWS - HEAD_DIM, tm), F32)
    for kv in range(KV_HEADS):
        vxa_ref[0, kv] = jnp.concatenate(
            [va_t[kv * HEAD_DIM:(kv + 1) * HEAD_DIM], ones], axis=0).astype(BF16)

    off = Q_W + 2 * KV_W
    qb_ref[0] = (normed[:, off:off + Q_W] * SM_SCALE).astype(BF16)
    kb = normed[:, off + Q_W:off + Q_W + KV_W]
    vb = qkv[:, off + Q_W + KV_W:off + Q_W + 2 * KV_W]
    kvb_ref[0] = jnp.concatenate(
        [kb, pltpu.roll(kb, HEAD_DIM, axis=1), vb, pltpu.roll(vb, HEAD_DIM, axis=1)],
        axis=1).astype(BF16)


def _qkv(x, mod, g_norm, w_qkv, gains, cos, sin, ones_bd, layer, tm):
    batch, seq, _ = x.shape
    return pl.pallas_call(
        _qkv_kernel,
        grid=(batch, seq // tm),
        in_specs=[
            pl.BlockSpec((1, tm, D_MODEL), lambda b, i: (b, i, 0)),
            pl.BlockSpec((1, N_MOD, D_MODEL), lambda b, i: (b, 0, 0)),
            pl.BlockSpec((1, 1, D_MODEL), lambda b, i: (3 * layer + 1, 0, 0)),
            pl.BlockSpec((1, D_MODEL, QKV_W), lambda b, i: (layer, 0, 0)),
            pl.BlockSpec((1, 1, QKV_W), lambda b, i: (layer, 0, 0)),
            pl.BlockSpec((tm, LANES), lambda b, i: (i, 0)),
            pl.BlockSpec((tm, LANES), lambda b, i: (i, 0)),
            pl.BlockSpec((MXU_DIM, MXU_DIM), lambda b, i: (0, 0)),
        ],
        out_specs=[
            pl.BlockSpec((1, Q_W, tm), lambda b, i: (b, 0, i)),
            pl.BlockSpec((1, tm, KV_W), lambda b, i: (b, i, 0)),
            pl.BlockSpec((1, KV_HEADS, VX_ROWS, tm), lambda b, i: (b, 0, 0, i)),
            pl.BlockSpec((1, tm, Q_W), lambda b, i: (b, i, 0)),
            pl.BlockSpec((1, tm, 4 * KV_W), lambda b, i: (b, i, 0)),
        ],
        out_shape=[
            jax.ShapeDtypeStruct((batch, Q_W, seq), BF16),
            jax.ShapeDtypeStruct((batch, seq, KV_W), BF16),
            jax.ShapeDtypeStruct((batch, KV_HEADS, VX_ROWS, seq), BF16),
            jax.ShapeDtypeStruct((batch, seq, Q_W), BF16),
            jax.ShapeDtypeStruct((batch, seq, 4 * KV_W), BF16),
        ],
        compiler_params=_params("parallel", "parallel"),
        name="qkv",
    )(x, mod, g_norm, w_qkv, gains, cos, sin, ones_bd)


def _attn_a_kernel(qt_ref, k_ref, vx_ref, o_ref, s_scr, *, seq, tk, tiles_per_trip):
    kv = pl.program_id(1)
    tq = qt_ref.shape[2]
    nq = GROUP * tq
    n_tiles = seq // tk
    q4 = qt_ref[0]
    qt = jnp.concatenate([q4[g * HEAD_DIM:(g + 1) * HEAD_DIM] for g in range(GROUP)], axis=1)
    zero = jnp.zeros_like(qt)
    q_pad = jnp.concatenate([jnp.where(kv == h, qt, zero) for h in range(KV_HEADS)], axis=0)

    def scores(j, slot, m):
        off = pl.multiple_of(jnp.minimum(j, n_tiles - 1) * tk, tk)
        s_t = _dot(k_ref[0, pl.ds(off, tk), :], q_pad)
        s_scr[slot] = s_t
        return jnp.maximum(m, jnp.max(s_t, axis=0, keepdims=True))

    def accumulate(j, slot, m_before, m_now, acc):
        off = pl.multiple_of(j * tk, tk)
        alpha = jnp.exp2(m_before - m_now)
        p_t = jnp.exp2(s_scr[slot] - m_now).astype(BF16)
        return alpha * acc + _dot(vx_ref[0, 0, :, pl.ds(off, tk)], p_t)

    def body(trip, carry):
        m_prev, m_cur, acc = carry
        j = tiles_per_trip * trip
        for u in range(tiles_per_trip):
            m_next = scores(j + u + 1, (u + 1) % 2, m_cur)
            acc = accumulate(j + u, u % 2, m_prev, m_cur, acc)
            m_prev, m_cur = m_cur, m_next
        return m_prev, m_cur, acc

    m0 = jnp.full((1, nq), -jnp.inf, F32)
    m1 = scores(0, 0, m0)
    acc0 = jnp.zeros((VX_ROWS, nq), F32)
    _, _, acc = lax.fori_loop(0, n_tiles // tiles_per_trip, body, (m0, m1, acc0))
    out_t = acc[:HEAD_DIM] / acc[HEAD_DIM:HEAD_DIM + 1]
    out_t = jnp.concatenate([out_t[:, g * tq:(g + 1) * tq] for g in range(GROUP)], axis=0)
    o_ref[0] = out_t.T.astype(BF16)


def _attn_a(qt_a, k_a, vx_a, tq, tk, tiles_per_trip):
    batch, _, seq = qt_a.shape
    gw = GROUP * HEAD_DIM
    assert tiles_per_trip % 2 == 0 and (seq // tk) % tiles_per_trip == 0
    return pl.pallas_call(
        functools.partial(_attn_a_kernel, seq=seq, tk=tk, tiles_per_trip=tiles_per_trip),
        grid=(batch, KV_HEADS, seq // tq),
        in_specs=[
            pl.BlockSpec((1, gw, tq), lambda b, h, i: (b, h, i)),
            pl.BlockSpec((1, seq, KV_W), lambda b, h, i: (b, 0, 0)),
            pl.BlockSpec((1, 1, VX_ROWS, seq), lambda b, h, i: (b, h, 0, 0)),
        ],
        out_specs=pl.BlockSpec((1, tq, gw), lambda b, h, i: (b, i, h)),
        out_shape=jax.ShapeDtypeStruct((batch, seq, Q_W), BF16),
        scratch_shapes=[pltpu.VMEM((2, tk, GROUP * tq), F32)],
        compiler_params=_params("parallel", "parallel", "parallel"),
        name="attn_global",
    )(qt_a, k_a, vx_a)


def _attn_b_kernel(sink_ref, q_ref, kp_ref, kc_ref, kn_ref, bias_ref, o_ref, *, seq, layer):
    n = pl.program_id(1)
    q = q_ref[0]
    band = jnp.concatenate([kp_ref[0], kc_ref[0], kn_ref[0]], axis=0)
    k_same, k_swap = band[:, 0:KV_W], band[:, KV_W:2 * KV_W]
    v_same, v_swap = band[:, 2 * KV_W:3 * KV_W], band[:, 3 * KV_W:4 * KV_W]

    qq = lax.broadcasted_iota(jnp.int32, (Q_BLOCK, BAND), 0)
    kk = lax.broadcasted_iota(jnp.int32, (Q_BLOCK, BAND), 1)
    key_pos = n * Q_BLOCK - WINDOW + kk
    rel = kk - WINDOW - qq
    visible = (rel >= -WINDOW) & (rel <= WINDOW) & (key_pos >= 0) & (key_pos < seq)
    lane = lax.broadcasted_iota(jnp.int32, (Q_BLOCK, LANES), 1)
    low = lane < HEAD_DIM
    zero = jnp.zeros((Q_BLOCK, LANES), BF16)

    for pair in range(Q_HEADS // 2):
        kv = (2 * pair) // GROUP
        q_pair = q[:, pair * LANES:(pair + 1) * LANES]
        halves = []
        for par in range(2):
            h = 2 * pair + par
            same = (kv == par)
            q_h = jnp.where(low, q_pair, zero) if par == 0 else jnp.where(low, zero, q_pair)
            s = _dot_nt(q_h, k_same if same else k_swap) + bias_ref[h]
            s = jnp.where(visible, s, -jnp.inf)
            sink = sink_ref[layer, h]
            m = jnp.maximum(jnp.max(s, axis=-1, keepdims=True), sink)
            p = jnp.exp(s - m)
            denom = jnp.sum(p, axis=-1, keepdims=True) + jnp.exp(sink - m)
            p = (p / denom).astype(BF16)
            halves.append(_dot(p, v_same if same else v_swap))
        o_ref[0, :, pair * LANES:(pair + 1) * LANES] = jnp.where(low, halves[0], halves[1]).astype(BF16)


def _attn_b(q_b, kv_b, band_bias, sinks, layer):
    batch, seq, _ = q_b.shape
    nb = seq // Q_BLOCK
    kv_spec = lambda f: pl.BlockSpec((1, Q_BLOCK, 4 * KV_W), f)
    return pl.pallas_call(
        functools.partial(_attn_b_kernel, seq=seq, layer=layer),
        grid=(batch, nb),
        in_specs=[
            pl.BlockSpec(memory_space=pltpu.SMEM),
            pl.BlockSpec((1, Q_BLOCK, Q_W), lambda b, n: (b, n, 0)),
            kv_spec(lambda b, n: (b, jnp.maximum(n - 1, 0), 0)),
            kv_spec(lambda b, n: (b, n, 0)),
            kv_spec(lambda b, n: (b, jnp.minimum(n + 1, nb - 1), 0)),
            pl.BlockSpec((Q_HEADS, Q_BLOCK, BAND), lambda b, n: (0, 0, 0)),
        ],
        out_specs=pl.BlockSpec((1, Q_BLOCK, Q_W), lambda b, n: (b, n, 0)),
        out_shape=jax.ShapeDtypeStruct((batch, seq, Q_W), BF16),
        compiler_params=_params("parallel", "parallel"),
        name="attn_window",
    )(sinks, q_b, kv_b, kv_b, kv_b, band_bias)


def _out_proj_kernel(x_ref, mod_ref, oa_ref, ob_ref, w_ref, o_ref):
    mix = jnp.concatenate([oa_ref[0], ob_ref[0]], axis=1)
    o_ref[0] = x_ref[0] + mod_ref[0, 5:6, :] * _dot(mix, w_ref[0])


def _out_proj(x, mod, out_a, out_b, w_o, layer, tm):
    batch, seq, _ = x.shape
    return pl.pallas_call(
        _out_proj_kernel,
        grid=(batch, seq // tm),
        in_specs=[
            pl.BlockSpec((1, tm, D_MODEL), lambda b, i: (b, i, 0)),
            pl.BlockSpec((1, N_MOD, D_MODEL), lambda b, i: (b, 0, 0)),
            pl.BlockSpec((1, tm, Q_W), lambda b, i: (b, i, 0)),
            pl.BlockSpec((1, tm, Q_W), lambda b, i: (b, i, 0)),
            pl.BlockSpec((1, 2 * Q_W, D_MODEL), lambda b, i: (layer, 0, 0)),
        ],
        out_specs=pl.BlockSpec((1, tm, D_MODEL), lambda b, i: (b, i, 0)),
        out_shape=jax.ShapeDtypeStruct(x.shape, F32),
        compiler_params=_params("parallel", "parallel"),
        name="out_proj",
    )(x, mod, out_a, out_b, w_o)


def _rope_tables(seq):
    pos = jnp.arange(seq)
    row = (pos // GRID_W).astype(F32)
    col = (pos % GRID_W).astype(F32)
    axis_dim = HEAD_DIM // 2
    freqs = ROPE_THETA ** (-jnp.arange(0, axis_dim, 2, dtype=F32) / axis_dim)
    ang_r = row[:, None] * freqs[None, :]
    ang_c = col[:, None] * freqs[None, :]
    cr, sr, cc, sc = jnp.cos(ang_r), jnp.sin(ang_r), jnp.cos(ang_c), jnp.sin(ang_c)
    cos = jnp.concatenate([cr, cr, cc, cc], axis=1)
    sin = jnp.concatenate([-sr, sr, -sc, sc], axis=1)
    reps = LANES // HEAD_DIM
    return jnp.tile(cos, (1, reps)), jnp.tile(sin, (1, reps))


def _qkv_gains(g_qk):
    ones = jnp.ones((DEPTH, KV_W), F32)
    t = lambda i, n: jnp.tile(g_qk[:, i, :].astype(F32), (1, n))
    return jnp.concatenate([t(0, Q_HEADS), t(1, KV_HEADS), ones,
                            t(2, Q_HEADS), t(3, KV_HEADS), ones], axis=1)[:, None, :]


def _tiles(seq):
    tm_ffn = min(1024, seq)
    tm_qkv = min(512, seq)
    tq = min(2 * Q_BLOCK, seq)
    tk = min(MXU_DIM, seq)
    tiles_per_trip = min(8, seq // tk)
    return tm_ffn, tm_qkv, tq, tk, tiles_per_trip


def _trunk(x, mod_all, weights, band_bias):
    seq = x.shape[1]
    tm_ffn, tm_qkv, tq, tk, tiles_per_trip = _tiles(seq)
    cos, sin = _rope_tables(seq)
    for l in range(DEPTH):
        mod = mod_all[l]
        x = _ffn(x, mod, weights["g_norm"], weights["ffn1_w_in"], weights["ffn1_w_out"], l, 0, tm_ffn, MXU_DIM)
        qt_a, k_a, vx_a, q_b, kv_b = _qkv(x, mod, weights["g_norm"], weights["w_qkv"], weights["gains"],
                                         cos, sin, weights["ones_bd"], l, tm_qkv)
        out_a = _attn_a(qt_a, k_a, vx_a, tq, tk, tiles_per_trip)
        out_b = _attn_b(q_b, kv_b, band_bias, weights["sinks"], l)
        x = _out_proj(x, mod, out_a, out_b, weights["w_o"], l, tm_qkv)
        x = _ffn(x, mod, weights["g_norm"], weights["ffn2_w_in"], weights["ffn2_w_out"], l, 2, tm_ffn, MXU_DIM)
    return x


def _prepare(w_ada, b_ada, g_norm, ffn1_w_in, ffn1_w_out, w_qkv, g_qk, attn_sinks, w_o,
             ffn2_w_in, ffn2_w_out):
    head_id = np.arange(MXU_DIM) // HEAD_DIM
    return {
        "g_norm": g_norm.astype(F32).reshape(DEPTH * 3, 1, D_MODEL),
        "ffn1_w_in": ffn1_w_in.astype(BF16), "ffn1_w_out": ffn1_w_out.astype(BF16),
        "ffn2_w_in": ffn2_w_in.astype(BF16), "ffn2_w_out": ffn2_w_out.astype(BF16),
        "w_qkv": w_qkv.astype(BF16), "w_o": w_o.astype(BF16),
        "gains": _qkv_gains(g_qk),
        "sinks": attn_sinks.astype(F32),
        "ones_bd": jnp.asarray(head_id[:, None] == head_id[None, :], BF16),
    }


def kernel(x_prompt, x_sample, c_prompt, c_sample, w_ada, b_ada, g_norm, ffn1_w_in, ffn1_w_out,
           w_qkv, g_qk, attn_sinks, rel_bias, w_o, ffn2_w_in, ffn2_w_out):
    weights = _prepare(w_ada, b_ada, g_norm, ffn1_w_in, ffn1_w_out, w_qkv, g_qk, attn_sinks, w_o,
                       ffn2_w_in, ffn2_w_out)
    nb_p, nb_s = c_prompt.shape[0], c_sample.shape[0]
    c_all = jnp.concatenate(
        [c_prompt, c_sample, jnp.zeros((C_ROWS - nb_p - nb_s, D_MODEL), F32)], axis=0)
    mod = _modulation(c_all, w_ada, b_ada).reshape(DEPTH, C_ROWS, N_MOD, D_MODEL)
    band_bias = _band_bias(rel_bias)
    y_prompt = _trunk(x_prompt, mod[:, :nb_p], weights, band_bias)
    y_sample = _trunk(x_sample, mod[:, nb_p:nb_p + nb_s], weights, band_bias)
    return (y_prompt, y_sample)
```

```python
import functools
import math

import jax
import jax.numpy as jnp
import numpy as np
from jax import lax
from jax.experimental import pallas as pl
from jax.experimental.pallas import tpu as pltpu

F32 = jnp.float32
BF16 = jnp.bfloat16

D_MODEL = 1024
DEPTH = 4
HEAD_DIM = 64
Q_HEADS = 8
KV_HEADS = 2
GROUP = Q_HEADS // KV_HEADS
Q_W = Q_HEADS * HEAD_DIM
KV_W = KV_HEADS * HEAD_DIM
QKV_W = 2 * (Q_W + 2 * KV_W)
D_FF = 2816
GRID_W = 64
Q_BLOCK = 128
WINDOW = 128
BAND = Q_BLOCK + 2 * WINDOW
NUM_BUCKETS = 32
MAX_DISTANCE = 128
ROPE_THETA = 10000.0
EPS = 1e-6
N_MOD = 9
SM_SCALE = HEAD_DIM ** -0.5
LOG2E = math.log2(math.e)
BF16_SUBLANES = 16
VX_ROWS = HEAD_DIM + BF16_SUBLANES

LANES = 128
MXU_DIM = 256
VMEM_LIMIT = 56 * 1024 * 1024

C_ROWS = 16


def _dot(a, b):
    return jnp.dot(a, b, preferred_element_type=F32)


def _silu(a):
    return a / (1.0 + jnp.exp(-a))


def _params(*sem):
    return pltpu.CompilerParams(dimension_semantics=sem, vmem_limit_bytes=VMEM_LIMIT)


def _mod_kernel(c_ref, w_ref, b_ref, o_ref):
    c = c_ref[...]
    o_ref[0] = _dot(_silu(c).astype(BF16), w_ref[0].astype(BF16)) + b_ref[0]


def _modulation(c_all, w_ada, b_ada):
    tn = D_MODEL
    n_out = N_MOD * D_MODEL
    return pl.pallas_call(
        _mod_kernel,
        grid=(DEPTH, n_out // tn),
        in_specs=[
            pl.BlockSpec((C_ROWS, D_MODEL), lambda l, j: (0, 0)),
            pl.BlockSpec((1, D_MODEL, tn), lambda l, j: (l, 0, j)),
            pl.BlockSpec((1, 1, tn), lambda l, j: (l, 0, j)),
        ],
        out_specs=pl.BlockSpec((1, C_ROWS, tn), lambda l, j: (l, 0, j)),
        out_shape=jax.ShapeDtypeStruct((DEPTH, C_ROWS, n_out), F32),
        compiler_params=_params("parallel", "parallel"),
        name="adaln_mod",
    )(c_all, w_ada, b_ada.reshape(DEPTH, 1, n_out))


def _bias_kernel(rb_ref, bucket_ref, o_ref):
    bucket = bucket_ref[...]
    for h in range(Q_HEADS):
        acc = jnp.full(bucket.shape, -jnp.inf, F32)
        for b in range(NUM_BUCKETS):
            acc = jnp.where(bucket == b, rb_ref[b, h] * LOG2E, acc)
        g = h % GROUP
        o_ref[h // GROUP, :, g * Q_BLOCK:(g + 1) * Q_BLOCK] = acc


def _t5_bucket(rel):
    half = NUM_BUCKETS // 2
    max_exact = half // 2
    base = jnp.where(rel > 0, half, 0)
    n = jnp.abs(rel)
    nf = jnp.maximum(n, 1).astype(F32)
    large = max_exact + (jnp.log(nf / max_exact) / math.log(MAX_DISTANCE / max_exact)
                         * (half - max_exact)).astype(jnp.int32)
    large = jnp.minimum(large, half - 1)
    return (base + jnp.where(n < max_exact, n, large)).astype(jnp.int32)


def _band_bias(rel_bias):
    kk = jnp.arange(BAND)[:, None]
    qq = jnp.arange(Q_BLOCK)[None, :]
    rel = kk - WINDOW - qq
    bucket = jnp.where(jnp.abs(rel) <= WINDOW, _t5_bucket(rel), -1)
    return pl.pallas_call(
        _bias_kernel,
        in_specs=[
            pl.BlockSpec(memory_space=pltpu.SMEM),
            pl.BlockSpec((BAND, Q_BLOCK), lambda: (0, 0)),
        ],
        out_specs=pl.BlockSpec((KV_HEADS, BAND, GROUP * Q_BLOCK), lambda: (0, 0, 0)),
        out_shape=jax.ShapeDtypeStruct((KV_HEADS, BAND, GROUP * Q_BLOCK), F32),
        name="band_bias",
    )(rel_bias.astype(F32), bucket)


def _ada_norm(x, g, scale, shift):
    y = x * lax.rsqrt(jnp.mean(x * x, axis=-1, keepdims=True) + EPS)
    return (y * g) * (1.0 + scale) + shift


def _ffn_kernel(x_ref, mod_ref, g_ref, wa_ref, wb_ref, wo_ref, o_ref, h_ref, acc_ref, *, rows, n_ff):
    j = pl.program_id(2)
    sh_i, sc_i, gt_i = rows

    @pl.when(j == 0)
    def _():
        h = _ada_norm(x_ref[0], g_ref[0], mod_ref[0, sc_i:sc_i + 1, :], mod_ref[0, sh_i:sh_i + 1, :])
        h_ref[...] = h.astype(BF16)
        acc_ref[...] = jnp.zeros(acc_ref.shape, F32)

    h = h_ref[...]
    a = _dot(h, wa_ref[0])
    b = _dot(h, wb_ref[0])
    acc_ref[...] += _dot((_silu(a) * b).astype(BF16), wo_ref[0])

    @pl.when(j == n_ff - 1)
    def _():
        o_ref[0] = x_ref[0] + 0.5 * mod_ref[0, gt_i:gt_i + 1, :] * acc_ref[...]


def _ffn(x, mod, g_norm, w_in, w_out, layer, sub, tm, tf):
    batch, seq, _ = x.shape
    n_ff = D_FF // tf
    rows = (3 * sub, 3 * sub + 1, 3 * sub + 2)
    return pl.pallas_call(
        functools.partial(_ffn_kernel, rows=rows, n_ff=n_ff),
        grid=(batch, seq // tm, n_ff),
        in_specs=[
            pl.BlockSpec((1, tm, D_MODEL), lambda b, i, j: (b, i, 0)),
            pl.BlockSpec((1, N_MOD, D_MODEL), lambda b, i, j: (b, 0, 0)),
            pl.BlockSpec((1, 1, D_MODEL), lambda b, i, j: (3 * layer + sub, 0, 0)),
            pl.BlockSpec((1, D_MODEL, tf), lambda b, i, j: (layer, 0, j)),
            pl.BlockSpec((1, D_MODEL, tf), lambda b, i, j: (layer, 0, j + n_ff)),
            pl.BlockSpec((1, tf, D_MODEL), lambda b, i, j: (layer, j, 0)),
        ],
        out_specs=pl.BlockSpec((1, tm, D_MODEL), lambda b, i, j: (b, i, 0)),
        out_shape=jax.ShapeDtypeStruct(x.shape, F32),
        scratch_shapes=[pltpu.VMEM((tm, D_MODEL), BF16), pltpu.VMEM((tm, D_MODEL), F32)],
        compiler_params=_params("parallel", "parallel", "arbitrary"),
        name="ffn",
    )(x, mod, g_norm, w_in, w_in, w_out)


def _head_sumsq(t, ones_bd):
    sq = t * t
    hi = sq.astype(BF16)
    lo = (sq - hi.astype(F32)).astype(BF16)
    return _dot(hi, ones_bd) + _dot(lo, ones_bd)


def _rope(t, cos, sin, first_half):
    up = pltpu.roll(t, LANES - HEAD_DIM // 4, axis=1)
    dn = pltpu.roll(t, HEAD_DIM // 4, axis=1)
    return t * cos + jnp.where(first_half, up, dn) * sin


def _store_vx(vx_ref, v_t):
    ones = jnp.ones((VX_ROWS - HEAD_DIM, v_t.shape[1]), F32)
    for kv in range(KV_HEADS):
        vx_ref[0, kv] = jnp.concatenate(
            [v_t[kv * HEAD_DIM:(kv + 1) * HEAD_DIM], ones], axis=0).astype(BF16)


def _qkv_kernel(x_ref, mod_ref, g_ref, w_ref, gain_ref, cos_ref, sin_ref, ones_ref,
                qta_ref, ka_ref, vxa_ref, qtb_ref, kb_ref, vxb_ref):
    tm = x_ref.shape[1]
    h = _ada_norm(x_ref[0], g_ref[0], mod_ref[0, 4:5, :], mod_ref[0, 3:4, :]).astype(BF16)
    qkv = _dot(h, w_ref[0])
    ones_bd = ones_ref[...]
    gain = gain_ref[0]

    normed = []
    for s in range(QKV_W // MXU_DIM):
        t = qkv[:, s * MXU_DIM:(s + 1) * MXU_DIM]
        ss = _head_sumsq(t, ones_bd)
        normed.append(t * lax.rsqrt(ss * (1.0 / HEAD_DIM) + EPS)
                      * gain[:, s * MXU_DIM:(s + 1) * MXU_DIM])
    normed = jnp.concatenate(normed, axis=1)

    cos = cos_ref[...]
    sin = sin_ref[...]
    lane = lax.broadcasted_iota(jnp.int32, (tm, LANES), 1)
    first_half = (lane % (HEAD_DIM // 2)) < (HEAD_DIM // 4)

    qa = [_rope(normed[:, s * LANES:(s + 1) * LANES], cos, sin, first_half) * (SM_SCALE * LOG2E)
          for s in range(Q_W // LANES)]
    qa = jnp.concatenate(qa, axis=1)
    qta_ref[0] = qa.T.astype(BF16)
    ka_ref[0] = _rope(normed[:, Q_W:Q_W + KV_W], cos, sin, first_half).astype(BF16)
    _store_vx(vxa_ref, qkv[:, Q_W + KV_W:Q_W + 2 * KV_W].T)

    off = Q_W + 2 * KV_W
    qtb_ref[0] = (normed[:, off:off + Q_W] * (SM_SCALE * LOG2E)).T.astype(BF16)
    kb_ref[0] = normed[:, off + Q_W:off + Q_W + KV_W].astype(BF16)
    _store_vx(vxb_ref, qkv[:, off + Q_W + KV_W:off + Q_W + 2 * KV_W].T)


def _qkv(x, mod, g_norm, w_qkv, gains, cos, sin, ones_bd, layer, tm):
    batch, seq, _ = x.shape
    return pl.pallas_call(
        _qkv_kernel,
        grid=(batch, seq // tm),
        in_specs=[
            pl.BlockSpec((1, tm, D_MODEL), lambda b, i: (b, i, 0)),
            pl.BlockSpec((1, N_MOD, D_MODEL), lambda b, i: (b, 0, 0)),
            pl.BlockSpec((1, 1, D_MODEL), lambda b, i: (3 * layer + 1, 0, 0)),
            pl.BlockSpec((1, D_MODEL, QKV_W), lambda b, i: (layer, 0, 0)),
            pl.BlockSpec((1, 1, QKV_W), lambda b, i: (layer, 0, 0)),
            pl.BlockSpec((tm, LANES), lambda b, i: (i, 0)),
            pl.BlockSpec((tm, LANES), lambda b, i: (i, 0)),
            pl.BlockSpec((MXU_DIM, MXU_DIM), lambda b, i: (0, 0)),
        ],
        out_specs=2 * [
            pl.BlockSpec((1, Q_W, tm), lambda b, i: (b, 0, i)),
            pl.BlockSpec((1, tm, KV_W), lambda b, i: (b, i, 0)),
            pl.BlockSpec((1, KV_HEADS, VX_ROWS, tm), lambda b, i: (b, 0, 0, i)),
        ],
        out_shape=2 * [
            jax.ShapeDtypeStruct((batch, Q_W, seq), BF16),
            jax.ShapeDtypeStruct((batch, seq, KV_W), BF16),
            jax.ShapeDtypeStruct((batch, KV_HEADS, VX_ROWS, seq), BF16),
        ],
        compiler_params=_params("parallel", "parallel"),
        name="qkv",
    )(x, mod, g_norm, w_qkv, gains, cos, sin, ones_bd)


def _attn_a_kernel(qt_ref, k_ref, vx_ref, o_ref, s_scr, *, seq, tk, tiles_per_trip):
    kv = pl.program_id(1)
    tq = qt_ref.shape[2]
    nq = GROUP * tq
    n_tiles = seq // tk
    q4 = qt_ref[0]
    qt = jnp.concatenate([q4[g * HEAD_DIM:(g + 1) * HEAD_DIM] for g in range(GROUP)], axis=1)
    zero = jnp.zeros_like(qt)
    q_pad = jnp.concatenate([jnp.where(kv == h, qt, zero) for h in range(KV_HEADS)], axis=0)

    def scores(j, slot, m):
        off = pl.multiple_of(jnp.minimum(j, n_tiles - 1) * tk, tk)
        s_t = _dot(k_ref[0, pl.ds(off, tk), :], q_pad)
        s_scr[slot] = s_t
        return jnp.maximum(m, jnp.max(s_t, axis=0, keepdims=True))

    def accumulate(j, slot, m_before, m_now, acc):
        off = pl.multiple_of(j * tk, tk)
        alpha = jnp.exp2(m_before - m_now)
        p_t = jnp.exp2(s_scr[slot] - m_now).astype(BF16)
        return alpha * acc + _dot(vx_ref[0, 0, :, pl.ds(off, tk)], p_t)

    def body(trip, carry):
        m_prev, m_cur, acc = carry
        j = tiles_per_trip * trip
        for u in range(tiles_per_trip):
            m_next = scores(j + u + 1, (u + 1) % 2, m_cur)
            acc = accumulate(j + u, u % 2, m_prev, m_cur, acc)
            m_prev, m_cur = m_cur, m_next
        return m_prev, m_cur, acc

    m0 = jnp.full((1, nq), -jnp.inf, F32)
    m1 = scores(0, 0, m0)
    acc0 = jnp.zeros((VX_ROWS, nq), F32)
    _, _, acc = lax.fori_loop(0, n_tiles // tiles_per_trip, body, (m0, m1, acc0))
    out_t = acc[:HEAD_DIM] / acc[HEAD_DIM:HEAD_DIM + 1]
    out_t = jnp.concatenate([out_t[:, g * tq:(g + 1) * tq] for g in range(GROUP)], axis=0)
    o_ref[0] = out_t.T.astype(BF16)


def _attn_a(qt_a, k_a, vx_a, tq, tk, tiles_per_trip):
    batch, _, seq = qt_a.shape
    gw = GROUP * HEAD_DIM
    assert tiles_per_trip % 2 == 0 and (seq // tk) % tiles_per_trip == 0
    return pl.pallas_call(
        functools.partial(_attn_a_kernel, seq=seq, tk=tk, tiles_per_trip=tiles_per_trip),
        grid=(batch, KV_HEADS, seq // tq),
        in_specs=[
            pl.BlockSpec((1, gw, tq), lambda b, h, i: (b, h, i)),
            pl.BlockSpec((1, seq, KV_W), lambda b, h, i: (b, 0, 0)),
            pl.BlockSpec((1, 1, VX_ROWS, seq), lambda b, h, i: (b, h, 0, 0)),
        ],
        out_specs=pl.BlockSpec((1, tq, gw), lambda b, h, i: (b, i, h)),
        out_shape=jax.ShapeDtypeStruct((batch, seq, Q_W), BF16),
        scratch_shapes=[pltpu.VMEM((2, tk, GROUP * tq), F32)],
        compiler_params=_params("parallel", "parallel", "parallel"),
        name="attn_global",
    )(qt_a, k_a, vx_a)


def _attn_b_kernel(sink_ref, qt_ref, kp_ref, kc_ref, kn_ref, vp_ref, vc_ref, vn_ref, bias_ref, o_ref, s_scr,
                   *, seq, layer, blocks):
    first_block = pl.program_id(1) * blocks
    n_blocks = seq // Q_BLOCK
    gw = GROUP * HEAD_DIM
    nq = GROUP * Q_BLOCK
    k_all = jnp.concatenate([kp_ref[0], kc_ref[0], kn_ref[0]], axis=0)
    lane_head = lax.broadcasted_iota(jnp.int32, (1, nq), 1) // Q_BLOCK
    zero = jnp.zeros((HEAD_DIM, nq), BF16)
    sinks = []
    for kv in range(KV_HEADS):
        sink = jnp.zeros((1, nq), F32)
        for g in range(GROUP):
            sink = jnp.where(lane_head == g, sink_ref[layer, kv * GROUP + g] * LOG2E, sink)
        sinks.append(sink)

    def scores(kv, blk, slot):
        lo = blk * Q_BLOCK
        q4 = qt_ref[0, kv * gw:(kv + 1) * gw, lo:lo + Q_BLOCK]
        qt = jnp.concatenate([q4[g * HEAD_DIM:(g + 1) * HEAD_DIM] for g in range(GROUP)], axis=1)
        q_pad = jnp.concatenate([qt, zero] if kv == 0 else [zero, qt], axis=0)
        s_t = _dot(k_all[lo:lo + BAND], q_pad) + bias_ref[kv]
        parts = [s_t[:Q_BLOCK], s_t[Q_BLOCK:2 * Q_BLOCK], s_t[2 * Q_BLOCK:]]
        if blk == 0:
            parts[0] = jnp.where(first_block > 0, parts[0], -jnp.inf)
        if blk == blocks - 1:
            parts[2] = jnp.where(first_block + blocks < n_blocks, parts[2], -jnp.inf)
        s_t = jnp.concatenate(parts, axis=0)
        s_scr[slot] = s_t
        return jnp.maximum(jnp.max(s_t, axis=0, keepdims=True), sinks[kv])

    def finish(kv, blk, slot, m):
        lo = blk * Q_BLOCK
        v_band = jnp.concatenate([vp_ref[0, kv], vc_ref[0, kv], vn_ref[0, kv]], axis=1)[:, lo:lo + BAND]
        p_t = jnp.exp2(s_scr[slot] - m).astype(BF16)
        acc = _dot(v_band, p_t)
        denom = acc[HEAD_DIM:HEAD_DIM + 1] + jnp.exp2(sinks[kv] - m)
        out_t = acc[:HEAD_DIM] / denom
        out_t = jnp.concatenate([out_t[:, g * Q_BLOCK:(g + 1) * Q_BLOCK] for g in range(GROUP)], axis=0)
        o_ref[0, lo:lo + Q_BLOCK, kv * gw:(kv + 1) * gw] = out_t.T.astype(BF16)

    chains = [(kv, blk) for kv in range(KV_HEADS) for blk in range(blocks)]
    m_next = scores(*chains[0], 0)
    for c, chain in enumerate(chains):
        m = m_next
        if c + 1 < len(chains):
            m_next = scores(*chains[c + 1], (c + 1) % 2)
        finish(*chain, c % 2, m)


def _attn_b(qt_b, k_b, vx_b, band_bias, sinks, layer, blocks):
    batch, _, seq = qt_b.shape
    nb = seq // Q_BLOCK
    tq = blocks * Q_BLOCK
    prev_blk = lambda i: jnp.maximum(i * blocks - 1, 0)
    next_blk = lambda i: jnp.minimum((i + 1) * blocks, nb - 1)
    return pl.pallas_call(
        functools.partial(_attn_b_kernel, seq=seq, layer=layer, blocks=blocks),
        grid=(batch, seq // tq),
        in_specs=[
            pl.BlockSpec(memory_space=pltpu.SMEM),
            pl.BlockSpec((1, Q_W, tq), lambda b, i: (b, 0, i)),
            pl.BlockSpec((1, Q_BLOCK, KV_W), lambda b, i: (b, prev_blk(i), 0)),
            pl.BlockSpec((1, tq, KV_W), lambda b, i: (b, i, 0)),
            pl.BlockSpec((1, Q_BLOCK, KV_W), lambda b, i: (b, next_blk(i), 0)),
            pl.BlockSpec((1, KV_HEADS, VX_ROWS, Q_BLOCK), lambda b, i: (b, 0, 0, prev_blk(i))),
            pl.BlockSpec((1, KV_HEADS, VX_ROWS, tq), lambda b, i: (b, 0, 0, i)),
            pl.BlockSpec((1, KV_HEADS, VX_ROWS, Q_BLOCK), lambda b, i: (b, 0, 0, next_blk(i))),
            pl.BlockSpec((KV_HEADS, BAND, GROUP * Q_BLOCK), lambda b, i: (0, 0, 0)),
        ],
        out_specs=pl.BlockSpec((1, tq, Q_W), lambda b, i: (b, i, 0)),
        out_shape=jax.ShapeDtypeStruct((batch, seq, Q_W), BF16),
        scratch_shapes=[pltpu.VMEM((2, BAND, GROUP * Q_BLOCK), F32)],
        compiler_params=_params("parallel", "parallel"),
        name="attn_window",
    )(sinks, qt_b, k_b, k_b, k_b, vx_b, vx_b, vx_b, band_bias)


def _out_proj_kernel(x_ref, mod_ref, oa_ref, ob_ref, w_ref, o_ref):
    mix = jnp.concatenate([oa_ref[0], ob_ref[0]], axis=1)
    o_ref[0] = x_ref[0] + mod_ref[0, 5:6, :] * _dot(mix, w_ref[0])


def _out_proj(x, mod, out_a, out_b, w_o, layer, tm):
    batch, seq, _ = x.shape
    return pl.pallas_call(
        _out_proj_kernel,
        grid=(batch, seq // tm),
        in_specs=[
            pl.BlockSpec((1, tm, D_MODEL), lambda b, i: (b, i, 0)),
            pl.BlockSpec((1, N_MOD, D_MODEL), lambda b, i: (b, 0, 0)),
            pl.BlockSpec((1, tm, Q_W), lambda b, i: (b, i, 0)),
            pl.BlockSpec((1, tm, Q_W), lambda b, i: (b, i, 0)),
            pl.BlockSpec((1, 2 * Q_W, D_MODEL), lambda b, i: (layer, 0, 0)),
        ],
        out_specs=pl.BlockSpec((1, tm, D_MODEL), lambda b, i: (b, i, 0)),
        out_shape=jax.ShapeDtypeStruct(x.shape, F32),
        compiler_params=_params("parallel", "parallel"),
        name="out_proj",
    )(x, mod, out_a, out_b, w_o)


def _rope_tables(seq):
    pos = jnp.arange(seq)
    row = (pos // GRID_W).astype(F32)
    col = (pos % GRID_W).astype(F32)
    axis_dim = HEAD_DIM // 2
    freqs = ROPE_THETA ** (-jnp.arange(0, axis_dim, 2, dtype=F32) / axis_dim)
    ang_r = row[:, None] * freqs[None, :]
    ang_c = col[:, None] * freqs[None, :]
    cr, sr, cc, sc = jnp.cos(ang_r), jnp.sin(ang_r), jnp.cos(ang_c), jnp.sin(ang_c)
    cos = jnp.concatenate([cr, cr, cc, cc], axis=1)
    sin = jnp.concatenate([-sr, sr, -sc, sc], axis=1)
    reps = LANES // HEAD_DIM
    return jnp.tile(cos, (1, reps)), jnp.tile(sin, (1, reps))


def _qkv_gains(g_qk):
    ones = jnp.ones((DEPTH, KV_W), F32)
    t = lambda i, n: jnp.tile(g_qk[:, i, :].astype(F32), (1, n))
    return jnp.concatenate([t(0, Q_HEADS), t(1, KV_HEADS), ones,
                            t(2, Q_HEADS), t(3, KV_HEADS), ones], axis=1)[:, None, :]


def _tiles(seq):
    tm_ffn = min(1024, seq)
    tm_qkv = min(512, seq)
    tq = min(2 * Q_BLOCK, seq)
    tk = min(MXU_DIM, seq)
    tiles_per_trip = min(8, seq // tk)
    win_blocks = min(8, seq // Q_BLOCK)
    return tm_ffn, tm_qkv, tq, tk, tiles_per_trip, win_blocks


def _trunk(x, mod_all, weights, band_bias):
    seq = x.shape[1]
    tm_ffn, tm_qkv, tq, tk, tiles_per_trip, win_blocks = _tiles(seq)
    cos, sin = _rope_tables(seq)
    for l in range(DEPTH):
        mod = mod_all[l]
        x = _ffn(x, mod, weights["g_norm"], weights["ffn1_w_in"], weights["ffn1_w_out"], l, 0, tm_ffn, MXU_DIM)
        qt_a, k_a, vx_a, qt_b, k_b, vx_b = _qkv(x, mod, weights["g_norm"], weights["w_qkv"], weights["gains"],
                                                cos, sin, weights["ones_bd"], l, tm_qkv)
        out_a = _attn_a(qt_a, k_a, vx_a, tq, tk, tiles_per_trip)
        out_b = _attn_b(qt_b, k_b, vx_b, band_bias, weights["sinks"], l, win_blocks)
        x = _out_proj(x, mod, out_a, out_b, weights["w_o"], l, tm_qkv)
        x = _ffn(x, mod, weights["g_norm"], weights["ffn2_w_in"], weights["ffn2_w_out"], l, 2, tm_ffn, MXU_DIM)
    return x


def _prepare(w_ada, b_ada, g_norm, ffn1_w_in, ffn1_w_out, w_qkv, g_qk, attn_sinks, w_o,
             ffn2_w_in, ffn2_w_out):
    head_id = np.arange(MXU_DIM) // HEAD_DIM
    return {
        "g_norm": g_norm.astype(F32).reshape(DEPTH * 3, 1, D_MODEL),
        "ffn1_w_in": ffn1_w_in.astype(BF16), "ffn1_w_out": ffn1_w_out.astype(BF16),
        "ffn2_w_in": ffn2_w_in.astype(BF16), "ffn2_w_out": ffn2_w_out.astype(BF16),
        "w_qkv": w_qkv.astype(BF16), "w_o": w_o.astype(BF16),
        "gains": _qkv_gains(g_qk),
        "sinks": attn_sinks.astype(F32),
        "ones_bd": jnp.asarray(head_id[:, None] == head_id[None, :], BF16),
    }


def kernel(x_prompt, x_sample, c_prompt, c_sample, w_ada, b_ada, g_norm, ffn1_w_in, ffn1_w_out,
           w_qkv, g_qk, attn_sinks, rel_bias, w_o, ffn2_w_in, ffn2_w_out):
    weights = _prepare(w_ada, b_ada, g_norm, ffn1_w_in, ffn1_w_out, w_qkv, g_qk, attn_sinks, w_o,
                       ffn2_w_in, ffn2_w_out)
    nb_p, nb_s = c_prompt.shape[0], c_sample.shape[0]
    c_all = jnp.concatenate(
        [c_prompt, c_sample, jnp.zeros((C_ROWS - nb_p - nb_s, D_MODEL), F32)], axis=0)
    mod = _modulation(c_all, w_ada, b_ada).reshape(DEPTH, C_ROWS, N_MOD, D_MODEL)
    band_bias = _band_bias(rel_bias)
    y_prompt = _trunk(x_prompt, mod[:, :nb_p], weights, band_bias)
    y_sample = _trunk(x_sample, mod[:, nb_p:nb_p + nb_s], weights, band_bias)
    return (y_prompt, y_sample)
```

```python
import functools
import math

import jax
import jax.numpy as jnp
import numpy as np
from jax import lax
from jax.experimental import pallas as pl
from jax.experimental.pallas import tpu as pltpu

F32 = jnp.float32
BF16 = jnp.bfloat16

D_MODEL = 1024
DEPTH = 4
HEAD_DIM = 64
Q_HEADS = 8
KV_HEADS = 2
GROUP = Q_HEADS // KV_HEADS
Q_W = Q_HEADS * HEAD_DIM
KV_W = KV_HEADS * HEAD_DIM
QKV_W = 2 * (Q_W + 2 * KV_W)
D_FF = 2816
GRID_W = 64
Q_BLOCK = 128
WINDOW = 128
BAND = Q_BLOCK + 2 * WINDOW
NUM_BUCKETS = 32
MAX_DISTANCE = 128
ROPE_THETA = 10000.0
EPS = 1e-6
N_MOD = 9
SM_SCALE = HEAD_DIM ** -0.5
LOG2E = math.log2(math.e)
BF16_SUBLANES = 16
VX_ROWS = HEAD_DIM + BF16_SUBLANES

LANES = 128
MXU_DIM = 256
VMEM_LIMIT = 56 * 1024 * 1024

C_ROWS = 16


def _dot(a, b):
    return jnp.dot(a, b, preferred_element_type=F32)


def _silu(a):
    return a / (1.0 + jnp.exp(-a))


def _params(*sem):
    return pltpu.CompilerParams(dimension_semantics=sem, vmem_limit_bytes=VMEM_LIMIT)


def _mod_kernel(c_ref, w_ref, b_ref, o_ref):
    c = c_ref[...]
    o_ref[0] = _dot(_silu(c).astype(BF16), w_ref[0].astype(BF16)) + b_ref[0]


def _modulation(c_all, w_ada, b_ada):
    tn = D_MODEL
    n_out = N_MOD * D_MODEL
    return pl.pallas_call(
        _mod_kernel,
        grid=(DEPTH, n_out // tn),
        in_specs=[
            pl.BlockSpec((C_ROWS, D_MODEL), lambda l, j: (0, 0)),
            pl.BlockSpec((1, D_MODEL, tn), lambda l, j: (l, 0, j)),
            pl.BlockSpec((1, 1, tn), lambda l, j: (l, 0, j)),
        ],
        out_specs=pl.BlockSpec((1, C_ROWS, tn), lambda l, j: (l, 0, j)),
        out_shape=jax.ShapeDtypeStruct((DEPTH, C_ROWS, n_out), F32),
        compiler_params=_params("parallel", "parallel"),
        name="adaln_mod",
    )(c_all, w_ada, b_ada.reshape(DEPTH, 1, n_out))


def _bias_kernel(rb_ref, bucket_ref, o_ref):
    bucket = bucket_ref[...]
    for h in range(Q_HEADS):
        acc = jnp.full(bucket.shape, -jnp.inf, F32)
        for b in range(NUM_BUCKETS):
            acc = jnp.where(bucket == b, rb_ref[b, h] * LOG2E, acc)
        g = h % GROUP
        o_ref[h // GROUP, :, g * Q_BLOCK:(g + 1) * Q_BLOCK] = acc


def _t5_bucket(rel):
    half = NUM_BUCKETS // 2
    max_exact = half // 2
    base = jnp.where(rel > 0, half, 0)
    n = jnp.abs(rel)
    nf = jnp.maximum(n, 1).astype(F32)
    large = max_exact + (jnp.log(nf / max_exact) / math.log(MAX_DISTANCE / max_exact)
                         * (half - max_exact)).astype(jnp.int32)
    large = jnp.minimum(large, half - 1)
    return (base + jnp.where(n < max_exact, n, large)).astype(jnp.int32)


def _band_bias(rel_bias):
    kk = jnp.arange(BAND)[:, None]
    qq = jnp.arange(Q_BLOCK)[None, :]
    rel = kk - WINDOW - qq
    bucket = jnp.where(jnp.abs(rel) <= WINDOW, _t5_bucket(rel), -1)
    return pl.pallas_call(
        _bias_kernel,
        in_specs=[
            pl.BlockSpec(memory_space=pltpu.SMEM),
            pl.BlockSpec((BAND, Q_BLOCK), lambda: (0, 0)),
        ],
        out_specs=pl.BlockSpec((KV_HEADS, BAND, GROUP * Q_BLOCK), lambda: (0, 0, 0)),
        out_shape=jax.ShapeDtypeStruct((KV_HEADS, BAND, GROUP * Q_BLOCK), F32),
        name="band_bias",
    )(rel_bias.astype(F32), bucket)


def _ada_norm(x, g, scale, shift):
    y = x * lax.rsqrt(jnp.mean(x * x, axis=-1, keepdims=True) + EPS)
    return y * (g * (1.0 + scale)) + shift


def _ffn_kernel(x_ref, mod_ref, g_ref, wi_ref, wo_ref, o_ref, *, rows):
    sh_i, sc_i, gt_i = rows
    x = x_ref[0]
    h = _ada_norm(x, g_ref[0], mod_ref[0, sc_i:sc_i + 1, :], mod_ref[0, sh_i:sh_i + 1, :]).astype(BF16)
    acc = jnp.zeros(x.shape, F32)
    for c in range(D_FF // MXU_DIM):
        lo = c * MXU_DIM
        a = _dot(h, wi_ref[0, :, lo:lo + MXU_DIM])
        b = _dot(h, wi_ref[0, :, D_FF + lo:D_FF + lo + MXU_DIM])
        acc = acc + _dot((_silu(a) * b).astype(BF16), wo_ref[0, lo:lo + MXU_DIM, :])
    o_ref[0] = x + 0.5 * mod_ref[0, gt_i:gt_i + 1, :] * acc


def _ffn(x, mod, g_norm, w_in, w_out, layer, sub, tm):
    batch, seq, _ = x.shape
    rows = (3 * sub, 3 * sub + 1, 3 * sub + 2)
    resident = pl.Buffered(1)
    return pl.pallas_call(
        functools.partial(_ffn_kernel, rows=rows),
        grid=(batch, seq // tm),
        in_specs=[
            pl.BlockSpec((1, tm, D_MODEL), lambda b, i: (b, i, 0)),
            pl.BlockSpec((1, N_MOD, D_MODEL), lambda b, i: (b, 0, 0)),
            pl.BlockSpec((1, 1, D_MODEL), lambda b, i: (3 * layer + sub, 0, 0)),
            pl.BlockSpec((1, D_MODEL, 2 * D_FF), lambda b, i: (layer, 0, 0), pipeline_mode=resident),
            pl.BlockSpec((1, D_FF, D_MODEL), lambda b, i: (layer, 0, 0), pipeline_mode=resident),
        ],
        out_specs=pl.BlockSpec((1, tm, D_MODEL), lambda b, i: (b, i, 0)),
        out_shape=jax.ShapeDtypeStruct(x.shape, F32),
        compiler_params=_params("parallel", "parallel"),
        name="ffn",
    )(x, mod, g_norm, w_in, w_out)


def _head_sumsq(t, ones_bd):
    sq = t * t
    hi = sq.astype(BF16)
    lo = (sq - hi.astype(F32)).astype(BF16)
    return _dot(hi, ones_bd) + _dot(lo, ones_bd)


def _rope(t, cos, sin, first_half):
    up = pltpu.roll(t, LANES - HEAD_DIM // 4, axis=1)
    dn = pltpu.roll(t, HEAD_DIM // 4, axis=1)
    return t * cos + jnp.where(first_half, up, dn) * sin


def _store_vx(vx_ref, v_t):
    ones = jnp.ones((VX_ROWS - HEAD_DIM, v_t.shape[1]), F32)
    for kv in range(KV_HEADS):
        vx_ref[0, kv] = jnp.concatenate(
            [v_t[kv * HEAD_DIM:(kv + 1) * HEAD_DIM], ones], axis=0).astype(BF16)


def _qkv_kernel(x_ref, mod_ref, g_ref, w_ref, gain_ref, cos_ref, sin_ref, ones_ref,
                qta_ref, ka_ref, vxa_ref, qtb_ref, kb_ref, vxb_ref):
    tm = x_ref.shape[1]
    h = _ada_norm(x_ref[0], g_ref[0], mod_ref[0, 4:5, :], mod_ref[0, 3:4, :]).astype(BF16)
    qkv = _dot(h, w_ref[0])
    ones_bd = ones_ref[...]
    gain = gain_ref[0]

    normed = []
    for s in range(QKV_W // MXU_DIM):
        t = qkv[:, s * MXU_DIM:(s + 1) * MXU_DIM]
        ss = _head_sumsq(t, ones_bd)
        normed.append(t * lax.rsqrt(ss * (1.0 / HEAD_DIM) + EPS)
                      * gain[:, s * MXU_DIM:(s + 1) * MXU_DIM])
    normed = jnp.concatenate(normed, axis=1)

    cos = cos_ref[...]
    sin = sin_ref[...]
    lane = lax.broadcasted_iota(jnp.int32, (tm, LANES), 1)
    first_half = (lane % (HEAD_DIM // 2)) < (HEAD_DIM // 4)

    qa = [_rope(normed[:, s * LANES:(s + 1) * LANES], cos, sin, first_half) * (SM_SCALE * LOG2E)
          for s in range(Q_W // LANES)]
    qa = jnp.concatenate(qa, axis=1)
    qta_ref[0] = qa.T.astype(BF16)
    ka_ref[0] = _rope(normed[:, Q_W:Q_W + KV_W], cos, sin, first_half).astype(BF16)
    _store_vx(vxa_ref, qkv[:, Q_W + KV_W:Q_W + 2 * KV_W].T)

    off = Q_W + 2 * KV_W
    qtb_ref[0] = (normed[:, off:off + Q_W] * (SM_SCALE * LOG2E)).T.astype(BF16)
    kb_ref[0] = normed[:, off + Q_W:off + Q_W + KV_W].astype(BF16)
    _store_vx(vxb_ref, qkv[:, off + Q_W + KV_W:off + Q_W + 2 * KV_W].T)


def _qkv(x, mod, g_norm, w_qkv, gains, cos, sin, ones_bd, layer, tm):
    batch, seq, _ = x.shape
    return pl.pallas_call(
        _qkv_kernel,
        grid=(batch, seq // tm),
        in_specs=[
            pl.BlockSpec((1, tm, D_MODEL), lambda b, i: (b, i, 0)),
            pl.BlockSpec((1, N_MOD, D_MODEL), lambda b, i: (b, 0, 0)),
            pl.BlockSpec((1, 1, D_MODEL), lambda b, i: (3 * layer + 1, 0, 0)),
            pl.BlockSpec((1, D_MODEL, QKV_W), lambda b, i: (layer, 0, 0)),
            pl.BlockSpec((1, 1, QKV_W), lambda b, i: (layer, 0, 0)),
            pl.BlockSpec((tm, LANES), lambda b, i: (i, 0)),
            pl.BlockSpec((tm, LANES), lambda b, i: (i, 0)),
            pl.BlockSpec((MXU_DIM, MXU_DIM), lambda b, i: (0, 0)),
        ],
        out_specs=2 * [
            pl.BlockSpec((1, Q_W, tm), lambda b, i: (b, 0, i)),
            pl.BlockSpec((1, tm, KV_W), lambda b, i: (b, i, 0)),
            pl.BlockSpec((1, KV_HEADS, VX_ROWS, tm), lambda b, i: (b, 0, 0, i)),
        ],
        out_shape=2 * [
            jax.ShapeDtypeStruct((batch, Q_W, seq), BF16),
            jax.ShapeDtypeStruct((batch, seq, KV_W), BF16),
            jax.ShapeDtypeStruct((batch, KV_HEADS, VX_ROWS, seq), BF16),
        ],
        compiler_params=_params("parallel", "parallel"),
        name="qkv",
    )(x, mod, g_norm, w_qkv, gains, cos, sin, ones_bd)


def _attn_a_kernel(qt_ref, k_ref, vx_ref, o_ref, s_scr, *, seq, tk, tiles_per_trip):
    kv = pl.program_id(1)
    tq = qt_ref.shape[2]
    nq = GROUP * tq
    n_tiles = seq // tk
    q4 = qt_ref[0]
    qt = jnp.concatenate([q4[g * HEAD_DIM:(g + 1) * HEAD_DIM] for g in range(GROUP)], axis=1)
    zero = jnp.zeros_like(qt)
    q_pad = jnp.concatenate([jnp.where(kv == h, qt, zero) for h in range(KV_HEADS)], axis=0)

    def scores(j, slot, m):
        off = pl.multiple_of(jnp.minimum(j, n_tiles - 1) * tk, tk)
        s_t = _dot(k_ref[0, pl.ds(off, tk), :], q_pad)
        s_scr[slot] = s_t
        return jnp.maximum(m, jnp.max(s_t, axis=0, keepdims=True))

    def accumulate(j, slot, m_before, m_now, acc):
        off = pl.multiple_of(j * tk, tk)
        alpha = jnp.exp2(m_before - m_now)
        p_t = jnp.exp2(s_scr[slot] - m_now).astype(BF16)
        return alpha * acc + _dot(vx_ref[0, 0, :, pl.ds(off, tk)], p_t)

    def body(trip, carry):
        m_prev, m_cur, acc = carry
        j = tiles_per_trip * trip
        for u in range(tiles_per_trip):
            m_next = scores(j + u + 1, (u + 1) % 2, m_cur)
            acc = accumulate(j + u, u % 2, m_prev, m_cur, acc)
            m_prev, m_cur = m_cur, m_next
        return m_prev, m_cur, acc

    m0 = jnp.full((1, nq), -jnp.inf, F32)
    m1 = scores(0, 0, m0)
    acc0 = jnp.zeros((VX_ROWS, nq), F32)
    _, _, acc = lax.fori_loop(0, n_tiles // tiles_per_trip, body, (m0, m1, acc0))
    out_t = acc[:HEAD_DIM] / acc[HEAD_DIM:HEAD_DIM + 1]
    out_t = jnp.concatenate([out_t[:, g * tq:(g + 1) * tq] for g in range(GROUP)], axis=0)
    o_ref[0] = out_t.T.astype(BF16)


def _attn_a(qt_a, k_a, vx_a, tq, tk, tiles_per_trip):
    batch, _, seq = qt_a.shape
    gw = GROUP * HEAD_DIM
    assert tiles_per_trip % 2 == 0 and (seq // tk) % tiles_per_trip == 0
    return pl.pallas_call(
        functools.partial(_attn_a_kernel, seq=seq, tk=tk, tiles_per_trip=tiles_per_trip),
        grid=(batch, KV_HEADS, seq // tq),
        in_specs=[
            pl.BlockSpec((1, gw, tq), lambda b, h, i: (b, h, i)),
            pl.BlockSpec((1, seq, KV_W), lambda b, h, i: (b, 0, 0)),
            pl.BlockSpec((1, 1, VX_ROWS, seq), lambda b, h, i: (b, h, 0, 0)),
        ],
        out_specs=pl.BlockSpec((1, tq, gw), lambda b, h, i: (b, i, h)),
        out_shape=jax.ShapeDtypeStruct((batch, seq, Q_W), BF16),
        scratch_shapes=[pltpu.VMEM((2, tk, GROUP * tq), F32)],
        compiler_params=_params("parallel", "parallel", "parallel"),
        name="attn_global",
    )(qt_a, k_a, vx_a)


def _attn_b_kernel(sink_ref, qt_ref, kp_ref, kc_ref, kn_ref, vp_ref, vc_ref, vn_ref, bias_ref, o_ref, s_scr,
                   *, seq, layer, blocks):
    first_block = pl.program_id(1) * blocks
    n_blocks = seq // Q_BLOCK
    gw = GROUP * HEAD_DIM
    nq = GROUP * Q_BLOCK
    k_all = jnp.concatenate([kp_ref[0], kc_ref[0], kn_ref[0]], axis=0)
    lane_head = lax.broadcasted_iota(jnp.int32, (1, nq), 1) // Q_BLOCK
    zero = jnp.zeros((HEAD_DIM, nq), BF16)
    sinks = []
    for kv in range(KV_HEADS):
        sink = jnp.zeros((1, nq), F32)
        for g in range(GROUP):
            sink = jnp.where(lane_head == g, sink_ref[layer, kv * GROUP + g] * LOG2E, sink)
        sinks.append(sink)

    def scores(kv, blk, slot):
        lo = blk * Q_BLOCK
        q4 = qt_ref[0, kv * gw:(kv + 1) * gw, lo:lo + Q_BLOCK]
        qt = jnp.concatenate([q4[g * HEAD_DIM:(g + 1) * HEAD_DIM] for g in range(GROUP)], axis=1)
        q_pad = jnp.concatenate([qt, zero] if kv == 0 else [zero, qt], axis=0)
        s_t = _dot(k_all[lo:lo + BAND], q_pad) + bias_ref[kv]
        parts = [s_t[:Q_BLOCK], s_t[Q_BLOCK:2 * Q_BLOCK], s_t[2 * Q_BLOCK:]]
        if blk == 0:
            parts[0] = jnp.where(first_block > 0, parts[0], -jnp.inf)
        if blk == blocks - 1:
            parts[2] = jnp.where(first_block + blocks < n_blocks, parts[2], -jnp.inf)
        s_t = jnp.concatenate(parts, axis=0)
        s_scr[slot] = s_t
        return jnp.maximum(jnp.max(s_t, axis=0, keepdims=True), sinks[kv])

    def finish(kv, blk, slot, m):
        lo = blk * Q_BLOCK
        v_band = jnp.concatenate([vp_ref[0, kv], vc_ref[0, kv], vn_ref[0, kv]], axis=1)[:, lo:lo + BAND]
        p_t = jnp.exp2(s_scr[slot] - m).astype(BF16)
        acc = _dot(v_band, p_t)
        denom = acc[HEAD_DIM:HEAD_DIM + 1] + jnp.exp2(sinks[kv] - m)
        out_t = acc[:HEAD_DIM] / denom
        out_t = jnp.concatenate([out_t[:, g * Q_BLOCK:(g + 1) * Q_BLOCK] for g in range(GROUP)], axis=0)
        o_ref[0, lo:lo + Q_BLOCK, kv * gw:(kv + 1) * gw] = out_t.T.astype(BF16)

    chains = [(kv, blk) for kv in range(KV_HEADS) for blk in range(blocks)]
    m_next = scores(*chains[0], 0)
    for c, chain in enumerate(chains):
        m = m_next
        if c + 1 < len(chains):
            m_next = scores(*chains[c + 1], (c + 1) % 2)
        finish(*chain, c % 2, m)


def _attn_b(qt_b, k_b, vx_b, band_bias, sinks, layer, blocks):
    batch, _, seq = qt_b.shape
    nb = seq // Q_BLOCK
    tq = blocks * Q_BLOCK
    prev_blk = lambda i: jnp.maximum(i * blocks - 1, 0)
    next_blk = lambda i: jnp.minimum((i + 1) * blocks, nb - 1)
    return pl.pallas_call(
        functools.partial(_attn_b_kernel, seq=seq, layer=layer, blocks=blocks),
        grid=(batch, seq // tq),
        in_specs=[
            pl.BlockSpec(memory_space=pltpu.SMEM),
            pl.BlockSpec((1, Q_W, tq), lambda b, i: (b, 0, i)),
            pl.BlockSpec((1, Q_BLOCK, KV_W), lambda b, i: (b, prev_blk(i), 0)),
            pl.BlockSpec((1, tq, KV_W), lambda b, i: (b, i, 0)),
            pl.BlockSpec((1, Q_BLOCK, KV_W), lambda b, i: (b, next_blk(i), 0)),
            pl.BlockSpec((1, KV_HEADS, VX_ROWS, Q_BLOCK), lambda b, i: (b, 0, 0, prev_blk(i))),
            pl.BlockSpec((1, KV_HEADS, VX_ROWS, tq), lambda b, i: (b, 0, 0, i)),
            pl.BlockSpec((1, KV_HEADS, VX_ROWS, Q_BLOCK), lambda b, i: (b, 0, 0, next_blk(i))),
            pl.BlockSpec((KV_HEADS, BAND, GROUP * Q_BLOCK), lambda b, i: (0, 0, 0)),
        ],
        out_specs=pl.BlockSpec((1, tq, Q_W), lambda b, i: (b, i, 0)),
        out_shape=jax.ShapeDtypeStruct((batch, seq, Q_W), BF16),
        scratch_shapes=[pltpu.VMEM((2, BAND, GROUP * Q_BLOCK), F32)],
        compiler_params=_params("parallel", "parallel"),
        name="attn_window",
    )(sinks, qt_b, k_b, k_b, k_b, vx_b, vx_b, vx_b, band_bias)


def _out_proj_kernel(x_ref, mod_ref, oa_ref, ob_ref, w_ref, o_ref):
    mix = jnp.concatenate([oa_ref[0], ob_ref[0]], axis=1)
    o_ref[0] = x_ref[0] + mod_ref[0, 5:6, :] * _dot(mix, w_ref[0])


def _out_proj(x, mod, out_a, out_b, w_o, layer, tm):
    batch, seq, _ = x.shape
    return pl.pallas_call(
        _out_proj_kernel,
        grid=(batch, seq // tm),
        in_specs=[
            pl.BlockSpec((1, tm, D_MODEL), lambda b, i: (b, i, 0)),
            pl.BlockSpec((1, N_MOD, D_MODEL), lambda b, i: (b, 0, 0)),
            pl.BlockSpec((1, tm, Q_W), lambda b, i: (b, i, 0)),
            pl.BlockSpec((1, tm, Q_W), lambda b, i: (b, i, 0)),
            pl.BlockSpec((1, 2 * Q_W, D_MODEL), lambda b, i: (layer, 0, 0)),
        ],
        out_specs=pl.BlockSpec((1, tm, D_MODEL), lambda b, i: (b, i, 0)),
        out_shape=jax.ShapeDtypeStruct(x.shape, F32),
        compiler_params=_params("parallel", "parallel"),
        name="out_proj",
    )(x, mod, out_a, out_b, w_o)


def _rope_tables(seq):
    pos = jnp.arange(seq)
    row = (pos // GRID_W).astype(F32)
    col = (pos % GRID_W).astype(F32)
    axis_dim = HEAD_DIM // 2
    freqs = ROPE_THETA ** (-jnp.arange(0, axis_dim, 2, dtype=F32) / axis_dim)
    ang_r = row[:, None] * freqs[None, :]
    ang_c = col[:, None] * freqs[None, :]
    cr, sr, cc, sc = jnp.cos(ang_r), jnp.sin(ang_r), jnp.cos(ang_c), jnp.sin(ang_c)
    cos = jnp.concatenate([cr, cr, cc, cc], axis=1)
    sin = jnp.concatenate([-sr, sr, -sc, sc], axis=1)
    reps = LANES // HEAD_DIM
    return jnp.tile(cos, (1, reps)), jnp.tile(sin, (1, reps))


def _qkv_gains(g_qk):
    ones = jnp.ones((DEPTH, KV_W), F32)
    t = lambda i, n: jnp.tile(g_qk[:, i, :].astype(F32), (1, n))
    return jnp.concatenate([t(0, Q_HEADS), t(1, KV_HEADS), ones,
                            t(2, Q_HEADS), t(3, KV_HEADS), ones], axis=1)[:, None, :]


def _tiles(seq):
    tm_ffn = min(512, seq)
    tm_qkv = min(512, seq)
    tq = min(2 * Q_BLOCK, seq)
    tk = min(MXU_DIM, seq)
    tiles_per_trip = min(16, seq // tk)
    win_blocks = min(8, seq // Q_BLOCK)
    return tm_ffn, tm_qkv, tq, tk, tiles_per_trip, win_blocks


def _trunk(x, mod_all, weights, band_bias):
    seq = x.shape[1]
    tm_ffn, tm_qkv, tq, tk, tiles_per_trip, win_blocks = _tiles(seq)
    cos, sin = _rope_tables(seq)
    for l in range(DEPTH):
        mod = mod_all[l]
        x = _ffn(x, mod, weights["g_norm"], weights["ffn1_w_in"], weights["ffn1_w_out"], l, 0, tm_ffn)
        qt_a, k_a, vx_a, qt_b, k_b, vx_b = _qkv(x, mod, weights["g_norm"], weights["w_qkv"], weights["gains"],
                                                cos, sin, weights["ones_bd"], l, tm_qkv)
        out_a = _attn_a(qt_a, k_a, vx_a, tq, tk, tiles_per_trip)
        out_b = _attn_b(qt_b, k_b, vx_b, band_bias, weights["sinks"], l, win_blocks)
        x = _out_proj(x, mod, out_a, out_b, weights["w_o"], l, tm_qkv)
        x = _ffn(x, mod, weights["g_norm"], weights["ffn2_w_in"], weights["ffn2_w_out"], l, 2, tm_ffn)
    return x


def _prepare(w_ada, b_ada, g_norm, ffn1_w_in, ffn1_w_out, w_qkv, g_qk, attn_sinks, w_o,
             ffn2_w_in, ffn2_w_out):
    head_id = np.arange(MXU_DIM) // HEAD_DIM
    return {
        "g_norm": g_norm.astype(F32).reshape(DEPTH * 3, 1, D_MODEL),
        "ffn1_w_in": ffn1_w_in.astype(BF16), "ffn1_w_out": ffn1_w_out.astype(BF16),
        "ffn2_w_in": ffn2_w_in.astype(BF16), "ffn2_w_out": ffn2_w_out.astype(BF16),
        "w_qkv": w_qkv.astype(BF16), "w_o": w_o.astype(BF16),
        "gains": _qkv_gains(g_qk),
        "sinks": attn_sinks.astype(F32),
        "ones_bd": jnp.asarray(head_id[:, None] == head_id[None, :], BF16),
    }


def kernel(x_prompt, x_sample, c_prompt, c_sample, w_ada, b_ada, g_norm, ffn1_w_in, ffn1_w_out,
           w_qkv, g_qk, attn_sinks, rel_bias, w_o, ffn2_w_in, ffn2_w_out):
    weights = _prepare(w_ada, b_ada, g_norm, ffn1_w_in, ffn1_w_out, w_qkv, g_qk, attn_sinks, w_o,
                       ffn2_w_in, ffn2_w_out)
    nb_p, nb_s = c_prompt.shape[0], c_sample.shape[0]
    c_all = jnp.concatenate(
        [c_prompt, c_sample, jnp.zeros((C_ROWS - nb_p - nb_s, D_MODEL), F32)], axis=0)
    mod = _modulation(c_all, w_ada, b_ada).reshape(DEPTH, C_ROWS, N_MOD, D_MODEL)
    band_bias = _band_bias(rel_bias)
    y_prompt = _trunk(x_prompt, mod[:, :nb_p], weights, band_bias)
    y_sample = _trunk(x_sample, mod[:, nb_p:nb_p + nb_s], weights, band_bias)
    return (y_prompt, y_sample)
```

```python
import functools
import math

import jax
import jax.numpy as jnp
import numpy as np
from jax import lax
from jax.experimental import pallas as pl
from jax.experimental.pallas import tpu as pltpu

F32 = jnp.float32
BF16 = jnp.bfloat16

D_MODEL = 1024
DEPTH = 4
HEAD_DIM = 64
Q_HEADS = 8
KV_HEADS = 2
GROUP = Q_HEADS // KV_HEADS
Q_W = Q_HEADS * HEAD_DIM
KV_W = KV_HEADS * HEAD_DIM
QKV_W = 2 * (Q_W + 2 * KV_W)
D_FF = 2816
GRID_W = 64
Q_BLOCK = 128
WINDOW = 128
BAND = Q_BLOCK + 2 * WINDOW
NUM_BUCKETS = 32
MAX_DISTANCE = 128
ROPE_THETA = 10000.0
EPS = 1e-6
N_MOD = 9
SM_SCALE = HEAD_DIM ** -0.5
LOG2E = math.log2(math.e)
BF16_SUBLANES = 16
VX_ROWS = HEAD_DIM + BF16_SUBLANES

LANES = 128
MXU_DIM = 256
VMEM_LIMIT = 56 * 1024 * 1024

C_ROWS = 16


def _dot(a, b):
    return jnp.dot(a, b, preferred_element_type=F32)


def _silu(a):
    return a / (1.0 + jnp.exp(-a))


def _params(*sem):
    return pltpu.CompilerParams(dimension_semantics=sem, vmem_limit_bytes=VMEM_LIMIT)


def _mod_kernel(c_ref, w_ref, b_ref, o_ref):
    c = c_ref[...]
    o_ref[0] = _dot(_silu(c).astype(BF16), w_ref[0].astype(BF16)) + b_ref[0]


def _modulation(c_all, w_ada, b_ada):
    tn = D_MODEL
    n_out = N_MOD * D_MODEL
    return pl.pallas_call(
        _mod_kernel,
        grid=(DEPTH, n_out // tn),
        in_specs=[
            pl.BlockSpec((C_ROWS, D_MODEL), lambda l, j: (0, 0)),
            pl.BlockSpec((1, D_MODEL, tn), lambda l, j: (l, 0, j)),
            pl.BlockSpec((1, 1, tn), lambda l, j: (l, 0, j)),
        ],
        out_specs=pl.BlockSpec((1, C_ROWS, tn), lambda l, j: (l, 0, j)),
        out_shape=jax.ShapeDtypeStruct((DEPTH, C_ROWS, n_out), F32),
        compiler_params=_params("parallel", "parallel"),
        name="adaln_mod",
    )(c_all, w_ada, b_ada.reshape(DEPTH, 1, n_out))


def _bias_kernel(rb_ref, bucket_ref, o_ref):
    bucket = bucket_ref[...]
    for h in range(Q_HEADS):
        acc = jnp.full(bucket.shape, -jnp.inf, F32)
        for b in range(NUM_BUCKETS):
            acc = jnp.where(bucket == b, rb_ref[b, h] * LOG2E, acc)
        g = h % GROUP
        o_ref[h // GROUP, :, g * Q_BLOCK:(g + 1) * Q_BLOCK] = acc


def _t5_bucket(rel):
    half = NUM_BUCKETS // 2
    max_exact = half // 2
    base = jnp.where(rel > 0, half, 0)
    n = jnp.abs(rel)
    nf = jnp.maximum(n, 1).astype(F32)
    large = max_exact + (jnp.log(nf / max_exact) / math.log(MAX_DISTANCE / max_exact)
                         * (half - max_exact)).astype(jnp.int32)
    large = jnp.minimum(large, half - 1)
    return (base + jnp.where(n < max_exact, n, large)).astype(jnp.int32)


def _band_bias(rel_bias):
    kk = jnp.arange(BAND)[:, None]
    qq = jnp.arange(Q_BLOCK)[None, :]
    rel = kk - WINDOW - qq
    bucket = jnp.where(jnp.abs(rel) <= WINDOW, _t5_bucket(rel), -1)
    return pl.pallas_call(
        _bias_kernel,
        in_specs=[
            pl.BlockSpec(memory_space=pltpu.SMEM),
            pl.BlockSpec((BAND, Q_BLOCK), lambda: (0, 0)),
        ],
        out_specs=pl.BlockSpec((KV_HEADS, BAND, GROUP * Q_BLOCK), lambda: (0, 0, 0)),
        out_shape=jax.ShapeDtypeStruct((KV_HEADS, BAND, GROUP * Q_BLOCK), F32),
        name="band_bias",
    )(rel_bias.astype(F32), bucket)


def _ada_norm(x, g, scale, shift):
    y = x * lax.rsqrt(jnp.mean(x * x, axis=-1, keepdims=True) + EPS)
    return y * (g * (1.0 + scale)) + shift


def _ffn_kernel(x_ref, mod_ref, g_ref, wi_ref, wo_ref, o_ref, *, rows):
    sh_i, sc_i, gt_i = rows
    x = x_ref[0]
    h = _ada_norm(x, g_ref[0], mod_ref[0, sc_i:sc_i + 1, :], mod_ref[0, sh_i:sh_i + 1, :]).astype(BF16)
    acc = jnp.zeros(x.shape, F32)
    for c in range(D_FF // MXU_DIM):
        lo = c * MXU_DIM
        a = _dot(h, wi_ref[0, :, lo:lo + MXU_DIM])
        b = _dot(h, wi_ref[0, :, D_FF + lo:D_FF + lo + MXU_DIM])
        acc = acc + _dot((_silu(a) * b).astype(BF16), wo_ref[0, lo:lo + MXU_DIM, :])
    o_ref[0] = x + 0.5 * mod_ref[0, gt_i:gt_i + 1, :] * acc


def _ffn(x, mod, g_norm, w_in, w_out, layer, sub, tm):
    batch, seq, _ = x.shape
    rows = (3 * sub, 3 * sub + 1, 3 * sub + 2)
    resident = pl.Buffered(1)
    return pl.pallas_call(
        functools.partial(_ffn_kernel, rows=rows),
        grid=(batch, seq // tm),
        in_specs=[
            pl.BlockSpec((1, tm, D_MODEL), lambda b, i: (b, i, 0)),
            pl.BlockSpec((1, N_MOD, D_MODEL), lambda b, i: (b, 0, 0)),
            pl.BlockSpec((1, 1, D_MODEL), lambda b, i: (3 * layer + sub, 0, 0)),
            pl.BlockSpec((1, D_MODEL, 2 * D_FF), lambda b, i: (layer, 0, 0), pipeline_mode=resident),
            pl.BlockSpec((1, D_FF, D_MODEL), lambda b, i: (layer, 0, 0), pipeline_mode=resident),
        ],
        out_specs=pl.BlockSpec((1, tm, D_MODEL), lambda b, i: (b, i, 0)),
        out_shape=jax.ShapeDtypeStruct(x.shape, F32),
        compiler_params=_params("parallel", "parallel"),
        name="ffn",
    )(x, mod, g_norm, w_in, w_out)


def _head_sumsq(t, ones_bd):
    sq = t * t
    hi = sq.astype(BF16)
    lo = (sq - hi.astype(F32)).astype(BF16)
    return _dot(hi, ones_bd) + _dot(lo, ones_bd)


def _rope(t, cos, sin, first_half):
    up = pltpu.roll(t, LANES - HEAD_DIM // 4, axis=1)
    dn = pltpu.roll(t, HEAD_DIM // 4, axis=1)
    return t * cos + jnp.where(first_half, up, dn) * sin


def _store_vx(vx_ref, v_t):
    ones = jnp.ones((VX_ROWS - HEAD_DIM, v_t.shape[1]), F32)
    for kv in range(KV_HEADS):
        vx_ref[0, kv] = jnp.concatenate(
            [v_t[kv * HEAD_DIM:(kv + 1) * HEAD_DIM], ones], axis=0).astype(BF16)


def _qkv_kernel(x_ref, mod_ref, g_ref, w_ref, gain_ref, cos_ref, sin_ref, ones_ref,
                qta_ref, ka_ref, vxa_ref, qtb_ref, kb_ref, vxb_ref):
    tm = x_ref.shape[1]
    h = _ada_norm(x_ref[0], g_ref[0], mod_ref[0, 4:5, :], mod_ref[0, 3:4, :]).astype(BF16)
    qkv = _dot(h, w_ref[0])
    ones_bd = ones_ref[...]
    gain = gain_ref[0]

    normed = []
    for s in range(QKV_W // MXU_DIM):
        t = qkv[:, s * MXU_DIM:(s + 1) * MXU_DIM]
        ss = _head_sumsq(t, ones_bd)
        normed.append(t * lax.rsqrt(ss * (1.0 / HEAD_DIM) + EPS)
                      * gain[:, s * MXU_DIM:(s + 1) * MXU_DIM])
    normed = jnp.concatenate(normed, axis=1)

    cos = cos_ref[...]
    sin = sin_ref[...]
    lane = lax.broadcasted_iota(jnp.int32, (tm, LANES), 1)
    first_half = (lane % (HEAD_DIM // 2)) < (HEAD_DIM // 4)

    qa = [_rope(normed[:, s * LANES:(s + 1) * LANES], cos, sin, first_half) * (SM_SCALE * LOG2E)
          for s in range(Q_W // LANES)]
    qa = jnp.concatenate(qa, axis=1)
    qta_ref[0] = qa.T.astype(BF16)
    ka_ref[0] = _rope(normed[:, Q_W:Q_W + KV_W], cos, sin, first_half).astype(BF16)
    _store_vx(vxa_ref, qkv[:, Q_W + KV_W:Q_W + 2 * KV_W].T)

    off = Q_W + 2 * KV_W
    qtb_ref[0] = (normed[:, off:off + Q_W] * (SM_SCALE * LOG2E)).T.astype(BF16)
    kb_ref[0] = normed[:, off + Q_W:off + Q_W + KV_W].astype(BF16)
    _store_vx(vxb_ref, qkv[:, off + Q_W + KV_W:off + Q_W + 2 * KV_W].T)


def _qkv(x, mod, g_norm, w_qkv, gains, cos, sin, ones_bd, layer, tm):
    batch, seq, _ = x.shape
    return pl.pallas_call(
        _qkv_kernel,
        grid=(batch, seq // tm),
        in_specs=[
            pl.BlockSpec((1, tm, D_MODEL), lambda b, i: (b, i, 0)),
            pl.BlockSpec((1, N_MOD, D_MODEL), lambda b, i: (b, 0, 0)),
            pl.BlockSpec((1, 1, D_MODEL), lambda b, i: (3 * layer + 1, 0, 0)),
            pl.BlockSpec((1, D_MODEL, QKV_W), lambda b, i: (layer, 0, 0)),
            pl.BlockSpec((1, 1, QKV_W), lambda b, i: (layer, 0, 0)),
            pl.BlockSpec((tm, LANES), lambda b, i: (i, 0)),
            pl.BlockSpec((tm, LANES), lambda b, i: (i, 0)),
            pl.BlockSpec((MXU_DIM, MXU_DIM), lambda b, i: (0, 0)),
        ],
        out_specs=2 * [
            pl.BlockSpec((1, Q_W, tm), lambda b, i: (b, 0, i)),
            pl.BlockSpec((1, tm, KV_W), lambda b, i: (b, i, 0)),
            pl.BlockSpec((1, KV_HEADS, VX_ROWS, tm), lambda b, i: (b, 0, 0, i)),
        ],
        out_shape=2 * [
            jax.ShapeDtypeStruct((batch, Q_W, seq), BF16),
            jax.ShapeDtypeStruct((batch, seq, KV_W), BF16),
            jax.ShapeDtypeStruct((batch, KV_HEADS, VX_ROWS, seq), BF16),
        ],
        compiler_params=_params("parallel", "parallel"),
        name="qkv",
    )(x, mod, g_norm, w_qkv, gains, cos, sin, ones_bd)


def _attn_a_kernel(qt_ref, k_ref, vx_ref, o_ref, s_scr, *, seq, tk, tiles_per_trip):
    kv = pl.program_id(1)
    tq = qt_ref.shape[2]
    nq = GROUP * tq
    n_tiles = seq // tk
    q4 = qt_ref[0]
    qt = jnp.concatenate([q4[g * HEAD_DIM:(g + 1) * HEAD_DIM] for g in range(GROUP)], axis=1)
    zero = jnp.zeros_like(qt)
    q_pad = jnp.concatenate([jnp.where(kv == h, qt, zero) for h in range(KV_HEADS)], axis=0)

    def scores(j, slot, m):
        off = pl.multiple_of(jnp.minimum(j, n_tiles - 1) * tk, tk)
        s_t = _dot(k_ref[0, pl.ds(off, tk), :], q_pad)
        s_scr[slot] = s_t
        return jnp.maximum(m, jnp.max(s_t, axis=0, keepdims=True))

    def accumulate(j, slot, m_before, m_now, acc):
        off = pl.multiple_of(j * tk, tk)
        alpha = jnp.exp2(m_before - m_now)
        p_t = jnp.exp2(s_scr[slot] - m_now).astype(BF16)
        return alpha * acc + _dot(vx_ref[0, 0, :, pl.ds(off, tk)], p_t)

    def qk(j):
        off = pl.multiple_of(jnp.minimum(j, n_tiles - 1) * tk, tk)
        return _dot(k_ref[0, pl.ds(off, tk), :], q_pad)

    def body(trip, acc):
        j = tiles_per_trip * trip
        s_cur = qk(j)
        for u in range(tiles_per_trip):
            if u + 1 < tiles_per_trip:
                s_next = qk(j + u + 1)
            off = pl.multiple_of((j + u) * tk, tk)
            p_t = jnp.exp2(s_cur).astype(BF16)
            acc = acc + _dot(vx_ref[0, 0, :, pl.ds(off, tk)], p_t)
            s_cur = s_next
        return acc

    acc0 = jnp.zeros((VX_ROWS, nq), F32)
    acc = lax.fori_loop(0, n_tiles // tiles_per_trip, body, acc0)
    out_t = acc[:HEAD_DIM] / acc[HEAD_DIM:HEAD_DIM + 1]
    out_t = jnp.concatenate([out_t[:, g * tq:(g + 1) * tq] for g in range(GROUP)], axis=0)
    o_ref[0] = out_t.T.astype(BF16)


def _attn_a(qt_a, k_a, vx_a, tq, tk, tiles_per_trip):
    batch, _, seq = qt_a.shape
    gw = GROUP * HEAD_DIM
    assert tiles_per_trip % 2 == 0 and (seq // tk) % tiles_per_trip == 0
    return pl.pallas_call(
        functools.partial(_attn_a_kernel, seq=seq, tk=tk, tiles_per_trip=tiles_per_trip),
        grid=(batch, KV_HEADS, seq // tq),
        in_specs=[
            pl.BlockSpec((1, gw, tq), lambda b, h, i: (b, h, i)),
            pl.BlockSpec((1, seq, KV_W), lambda b, h, i: (b, 0, 0)),
            pl.BlockSpec((1, 1, VX_ROWS, seq), lambda b, h, i: (b, h, 0, 0)),
        ],
        out_specs=pl.BlockSpec((1, tq, gw), lambda b, h, i: (b, i, h)),
        out_shape=jax.ShapeDtypeStruct((batch, seq, Q_W), BF16),
        scratch_shapes=[pltpu.VMEM((2, tk, GROUP * tq), F32)],
        compiler_params=_params("parallel", "parallel", "parallel"),
        name="attn_global",
    )(qt_a, k_a, vx_a)


def _attn_b_kernel(sink_ref, qt_ref, kp_ref, kc_ref, kn_ref, vp_ref, vc_ref, vn_ref, bias_ref, o_ref, s_scr,
                   *, seq, layer, blocks):
    first_block = pl.program_id(1) * blocks
    n_blocks = seq // Q_BLOCK
    gw = GROUP * HEAD_DIM
    nq = GROUP * Q_BLOCK
    k_all = jnp.concatenate([kp_ref[0], kc_ref[0], kn_ref[0]], axis=0)
    lane_head = lax.broadcasted_iota(jnp.int32, (1, nq), 1) // Q_BLOCK
    zero = jnp.zeros((HEAD_DIM, nq), BF16)
    sinks = []
    for kv in range(KV_HEADS):
        sink = jnp.zeros((1, nq), F32)
        for g in range(GROUP):
            sink = jnp.where(lane_head == g, sink_ref[layer, kv * GROUP + g] * LOG2E, sink)
        sinks.append(sink)

    def scores(kv, blk, slot):
        lo = blk * Q_BLOCK
        q4 = qt_ref[0, kv * gw:(kv + 1) * gw, lo:lo + Q_BLOCK]
        qt = jnp.concatenate([q4[g * HEAD_DIM:(g + 1) * HEAD_DIM] for g in range(GROUP)], axis=1)
        q_pad = jnp.concatenate([qt, zero] if kv == 0 else [zero, qt], axis=0)
        s_t = _dot(k_all[lo:lo + BAND], q_pad) + bias_ref[kv]
        parts = [s_t[:Q_BLOCK], s_t[Q_BLOCK:2 * Q_BLOCK], s_t[2 * Q_BLOCK:]]
        if blk == 0:
            parts[0] = jnp.where(first_block > 0, parts[0], -jnp.inf)
        if blk == blocks - 1:
            parts[2] = jnp.where(first_block + blocks < n_blocks, parts[2], -jnp.inf)
        s_t = jnp.concatenate(parts, axis=0)
        s_scr[slot] = s_t
        return jnp.maximum(jnp.max(s_t, axis=0, keepdims=True), sinks[kv])

    def finish(kv, blk, slot, m):
        lo = blk * Q_BLOCK
        v_band = jnp.concatenate([vp_ref[0, kv], vc_ref[0, kv], vn_ref[0, kv]], axis=1)[:, lo:lo + BAND]
        p_t = jnp.exp2(s_scr[slot] - m).astype(BF16)
        acc = _dot(v_band, p_t)
        denom = acc[HEAD_DIM:HEAD_DIM + 1] + jnp.exp2(sinks[kv] - m)
        out_t = acc[:HEAD_DIM] / denom
        out_t = jnp.concatenate([out_t[:, g * Q_BLOCK:(g + 1) * Q_BLOCK] for g in range(GROUP)], axis=0)
        o_ref[0, lo:lo + Q_BLOCK, kv * gw:(kv + 1) * gw] = out_t.T.astype(BF16)

    chains = [(kv, blk) for kv in range(KV_HEADS) for blk in range(blocks)]
    m_next = scores(*chains[0], 0)
    for c, chain in enumerate(chains):
        m = m_next
        if c + 1 < len(chains):
            m_next = scores(*chains[c + 1], (c + 1) % 2)
        finish(*chain, c % 2, m)


def _attn_b(qt_b, k_b, vx_b, band_bias, sinks, layer, blocks):
    batch, _, seq = qt_b.shape
    nb = seq // Q_BLOCK
    tq = blocks * Q_BLOCK
    prev_blk = lambda i: jnp.maximum(i * blocks - 1, 0)
    next_blk = lambda i: jnp.minimum((i + 1) * blocks, nb - 1)
    return pl.pallas_call(
        functools.partial(_attn_b_kernel, seq=seq, layer=layer, blocks=blocks),
        grid=(batch, seq // tq),
        in_specs=[
            pl.BlockSpec(memory_space=pltpu.SMEM),
            pl.BlockSpec((1, Q_W, tq), lambda b, i: (b, 0, i)),
            pl.BlockSpec((1, Q_BLOCK, KV_W), lambda b, i: (b, prev_blk(i), 0)),
            pl.BlockSpec((1, tq, KV_W), lambda b, i: (b, i, 0)),
            pl.BlockSpec((1, Q_BLOCK, KV_W), lambda b, i: (b, next_blk(i), 0)),
            pl.BlockSpec((1, KV_HEADS, VX_ROWS, Q_BLOCK), lambda b, i: (b, 0, 0, prev_blk(i))),
            pl.BlockSpec((1, KV_HEADS, VX_ROWS, tq), lambda b, i: (b, 0, 0, i)),
            pl.BlockSpec((1, KV_HEADS, VX_ROWS, Q_BLOCK), lambda b, i: (b, 0, 0, next_blk(i))),
            pl.BlockSpec((KV_HEADS, BAND, GROUP * Q_BLOCK), lambda b, i: (0, 0, 0)),
        ],
        out_specs=pl.BlockSpec((1, tq, Q_W), lambda b, i: (b, i, 0)),
        out_shape=jax.ShapeDtypeStruct((batch, seq, Q_W), BF16),
        scratch_shapes=[pltpu.VMEM((2, BAND, GROUP * Q_BLOCK), F32)],
        compiler_params=_params("parallel", "parallel"),
        name="attn_window",
    )(sinks, qt_b, k_b, k_b, k_b, vx_b, vx_b, vx_b, band_bias)


def _out_proj_kernel(x_ref, mod_ref, oa_ref, ob_ref, w_ref, o_ref):
    mix = jnp.concatenate([oa_ref[0], ob_ref[0]], axis=1)
    o_ref[0] = x_ref[0] + mod_ref[0, 5:6, :] * _dot(mix, w_ref[0])


def _out_proj(x, mod, out_a, out_b, w_o, layer, tm):
    batch, seq, _ = x.shape
    return pl.pallas_call(
        _out_proj_kernel,
        grid=(batch, seq // tm),
        in_specs=[
            pl.BlockSpec((1, tm, D_MODEL), lambda b, i: (b, i, 0)),
            pl.BlockSpec((1, N_MOD, D_MODEL), lambda b, i: (b, 0, 0)),
            pl.BlockSpec((1, tm, Q_W), lambda b, i: (b, i, 0)),
            pl.BlockSpec((1, tm, Q_W), lambda b, i: (b, i, 0)),
            pl.BlockSpec((1, 2 * Q_W, D_MODEL), lambda b, i: (layer, 0, 0)),
        ],
        out_specs=pl.BlockSpec((1, tm, D_MODEL), lambda b, i: (b, i, 0)),
        out_shape=jax.ShapeDtypeStruct(x.shape, F32),
        compiler_params=_params("parallel", "parallel"),
        name="out_proj",
    )(x, mod, out_a, out_b, w_o)


def _rope_tables(seq):
    pos = jnp.arange(seq)
    row = (pos // GRID_W).astype(F32)
    col = (pos % GRID_W).astype(F32)
    axis_dim = HEAD_DIM // 2
    freqs = ROPE_THETA ** (-jnp.arange(0, axis_dim, 2, dtype=F32) / axis_dim)
    ang_r = row[:, None] * freqs[None, :]
    ang_c = col[:, None] * freqs[None, :]
    cr, sr, cc, sc = jnp.cos(ang_r), jnp.sin(ang_r), jnp.cos(ang_c), jnp.sin(ang_c)
    cos = jnp.concatenate([cr, cr, cc, cc], axis=1)
    sin = jnp.concatenate([-sr, sr, -sc, sc], axis=1)
    reps = LANES // HEAD_DIM
    return jnp.tile(cos, (1, reps)), jnp.tile(sin, (1, reps))


def _qkv_gains(g_qk):
    ones = jnp.ones((DEPTH, KV_W), F32)
    t = lambda i, n: jnp.tile(g_qk[:, i, :].astype(F32), (1, n))
    return jnp.concatenate([t(0, Q_HEADS), t(1, KV_HEADS), ones,
                            t(2, Q_HEADS), t(3, KV_HEADS), ones], axis=1)[:, None, :]


def _tiles(seq):
    tm_ffn = min(512, seq)
    tm_qkv = min(512, seq)
    tq = min(2 * Q_BLOCK, seq)
    tk = min(MXU_DIM, seq)
    tiles_per_trip = min(16, seq // tk)
    win_blocks = min(8, seq // Q_BLOCK)
    return tm_ffn, tm_qkv, tq, tk, tiles_per_trip, win_blocks


def _trunk(x, mod_all, weights, band_bias):
    seq = x.shape[1]
    tm_ffn, tm_qkv, tq, tk, tiles_per_trip, win_blocks = _tiles(seq)
    cos, sin = _rope_tables(seq)
    for l in range(DEPTH):
        mod = mod_all[l]
        x = _ffn(x, mod, weights["g_norm"], weights["ffn1_w_in"], weights["ffn1_w_out"], l, 0, tm_ffn)
        qt_a, k_a, vx_a, qt_b, k_b, vx_b = _qkv(x, mod, weights["g_norm"], weights["w_qkv"], weights["gains"],
                                                cos, sin, weights["ones_bd"], l, tm_qkv)
        out_a = _attn_a(qt_a, k_a, vx_a, tq, tk, tiles_per_trip)
        out_b = _attn_b(qt_b, k_b, vx_b, band_bias, weights["sinks"], l, win_blocks)
        x = _out_proj(x, mod, out_a, out_b, weights["w_o"], l, tm_qkv)
        x = _ffn(x, mod, weights["g_norm"], weights["ffn2_w_in"], weights["ffn2_w_out"], l, 2, tm_ffn)
    return x


def _prepare(w_ada, b_ada, g_norm, ffn1_w_in, ffn1_w_out, w_qkv, g_qk, attn_sinks, w_o,
             ffn2_w_in, ffn2_w_out):
    head_id = np.arange(MXU_DIM) // HEAD_DIM
    return {
        "g_norm": g_norm.astype(F32).reshape(DEPTH * 3, 1, D_MODEL),
        "ffn1_w_in": ffn1_w_in.astype(BF16), "ffn1_w_out": ffn1_w_out.astype(BF16),
        "ffn2_w_in": ffn2_w_in.astype(BF16), "ffn2_w_out": ffn2_w_out.astype(BF16),
        "w_qkv": w_qkv.astype(BF16), "w_o": w_o.astype(BF16),
        "gains": _qkv_gains(g_qk),
        "sinks": attn_sinks.astype(F32),
        "ones_bd": jnp.asarray(head_id[:, None] == head_id[None, :], BF16),
    }


def kernel(x_prompt, x_sample, c_prompt, c_sample, w_ada, b_ada, g_norm, ffn1_w_in, ffn1_w_out,
           w_qkv, g_qk, attn_sinks, rel_bias, w_o, ffn2_w_in, ffn2_w_out):
    weights = _prepare(w_ada, b_ada, g_norm, ffn1_w_in, ffn1_w_out, w_qkv, g_qk, attn_sinks, w_o,
                       ffn2_w_in, ffn2_w_out)
    nb_p, nb_s = c_prompt.shape[0], c_sample.shape[0]
    c_all = jnp.concatenate(
        [c_prompt, c_sample, jnp.zeros((C_ROWS - nb_p - nb_s, D_MODEL), F32)], axis=0)
    mod = _modulation(c_all, w_ada, b_ada).reshape(DEPTH, C_ROWS, N_MOD, D_MODEL)
    band_bias = _band_bias(rel_bias)
    y_prompt = _trunk(x_prompt, mod[:, :nb_p], weights, band_bias)
    y_sample = _trunk(x_sample, mod[:, nb_p:nb_p + nb_s], weights, band_bias)
    return (y_prompt, y_sample)
```

```python
import functools
import math

import jax
import jax.numpy as jnp
import numpy as np
from jax import lax
from jax.experimental import pallas as pl
from jax.experimental.pallas import tpu as pltpu

F32 = jnp.float32
BF16 = jnp.bfloat16

D_MODEL = 1024
DEPTH = 4
HEAD_DIM = 64
Q_HEADS = 8
KV_HEADS = 2
GROUP = Q_HEADS // KV_HEADS
Q_W = Q_HEADS * HEAD_DIM
KV_W = KV_HEADS * HEAD_DIM
QKV_W = 2 * (Q_W + 2 * KV_W)
D_FF = 2816
GRID_W = 64
Q_BLOCK = 128
WINDOW = 128
BAND = Q_BLOCK + 2 * WINDOW
NUM_BUCKETS = 32
MAX_DISTANCE = 128
ROPE_THETA = 10000.0
EPS = 1e-6
N_MOD = 9
SM_SCALE = HEAD_DIM ** -0.5
LOG2E = math.log2(math.e)
BF16_SUBLANES = 16
VX_ROWS = HEAD_DIM + BF16_SUBLANES

LANES = 128
MXU_DIM = 256
VMEM_LIMIT = 56 * 1024 * 1024

C_ROWS = 16


def _dot(a, b):
    return jnp.dot(a, b, preferred_element_type=F32)


def _silu(a):
    return a / (1.0 + jnp.exp(-a))


def _params(*sem):
    return pltpu.CompilerParams(dimension_semantics=sem, vmem_limit_bytes=VMEM_LIMIT)


def _mod_kernel(c_ref, w_ref, b_ref, o_ref):
    c = c_ref[...]
    o_ref[0] = _dot(_silu(c).astype(BF16), w_ref[0].astype(BF16)) + b_ref[0]


def _modulation(c_all, w_ada, b_ada):
    tn = D_MODEL
    n_out = N_MOD * D_MODEL
    return pl.pallas_call(
        _mod_kernel,
        grid=(DEPTH, n_out // tn),
        in_specs=[
            pl.BlockSpec((C_ROWS, D_MODEL), lambda l, j: (0, 0)),
            pl.BlockSpec((1, D_MODEL, tn), lambda l, j: (l, 0, j)),
            pl.BlockSpec((1, 1, tn), lambda l, j: (l, 0, j)),
        ],
        out_specs=pl.BlockSpec((1, C_ROWS, tn), lambda l, j: (l, 0, j)),
        out_shape=jax.ShapeDtypeStruct((DEPTH, C_ROWS, n_out), F32),
        compiler_params=_params("parallel", "parallel"),
        name="adaln_mod",
    )(c_all, w_ada, b_ada.reshape(DEPTH, 1, n_out))


def _bias_kernel(rb_ref, bucket_ref, o_ref):
    bucket = bucket_ref[...]
    for h in range(Q_HEADS):
        acc = jnp.full(bucket.shape, -jnp.inf, F32)
        for b in range(NUM_BUCKETS):
            acc = jnp.where(bucket == b, rb_ref[b, h] * LOG2E, acc)
        g = h % GROUP
        o_ref[h // GROUP, :, g * Q_BLOCK:(g + 1) * Q_BLOCK] = acc


def _t5_bucket(rel):
    half = NUM_BUCKETS // 2
    max_exact = half // 2
    base = jnp.where(rel > 0, half, 0)
    n = jnp.abs(rel)
    nf = jnp.maximum(n, 1).astype(F32)
    large = max_exact + (jnp.log(nf / max_exact) / math.log(MAX_DISTANCE / max_exact)
                         * (half - max_exact)).astype(jnp.int32)
    large = jnp.minimum(large, half - 1)
    return (base + jnp.where(n < max_exact, n, large)).astype(jnp.int32)


def _band_bias(rel_bias):
    kk = jnp.arange(BAND)[:, None]
    qq = jnp.arange(Q_BLOCK)[None, :]
    rel = kk - WINDOW - qq
    bucket = jnp.where(jnp.abs(rel) <= WINDOW, _t5_bucket(rel), -1)
    return pl.pallas_call(
        _bias_kernel,
        in_specs=[
            pl.BlockSpec(memory_space=pltpu.SMEM),
            pl.BlockSpec((BAND, Q_BLOCK), lambda: (0, 0)),
        ],
        out_specs=pl.BlockSpec((KV_HEADS, BAND, GROUP * Q_BLOCK), lambda: (0, 0, 0)),
        out_shape=jax.ShapeDtypeStruct((KV_HEADS, BAND, GROUP * Q_BLOCK), F32),
        name="band_bias",
    )(rel_bias.astype(F32), bucket)


def _ada_norm(x, g, scale, shift):
    y = x * lax.rsqrt(jnp.mean(x * x, axis=-1, keepdims=True) + EPS)
    return y * (g * (1.0 + scale)) + shift


def _ffn_kernel(x_ref, mod_ref, g_ref, wi_ref, wo_ref, *rest, rows, mix_gate_row):
    sh_i, sc_i, gt_i = rows
    x = x_ref[0]
    if mix_gate_row is None:
        (o_ref,) = rest
    else:
        oa_ref, ob_ref, wmix_ref, o_ref = rest
        mix = jnp.concatenate([oa_ref[0], ob_ref[0]], axis=1)
        x = x + mod_ref[0, mix_gate_row:mix_gate_row + 1, :] * _dot(mix, wmix_ref[0])
    h = _ada_norm(x, g_ref[0], mod_ref[0, sc_i:sc_i + 1, :], mod_ref[0, sh_i:sh_i + 1, :]).astype(BF16)
    acc = jnp.zeros(x.shape, F32)
    for c in range(D_FF // MXU_DIM):
        lo = c * MXU_DIM
        a = _dot(h, wi_ref[0, :, lo:lo + MXU_DIM])
        b = _dot(h, wi_ref[0, :, D_FF + lo:D_FF + lo + MXU_DIM])
        acc = acc + _dot((_silu(a) * b).astype(BF16), wo_ref[0, lo:lo + MXU_DIM, :])
    o_ref[0] = x + 0.5 * mod_ref[0, gt_i:gt_i + 1, :] * acc


def _ffn(x, mod, g_norm, w_in, w_out, layer, sub, tm, mix=None):
    batch, seq, _ = x.shape
    rows = (3 * sub, 3 * sub + 1, 3 * sub + 2)
    resident = pl.Buffered(1)
    in_specs = [
        pl.BlockSpec((1, tm, D_MODEL), lambda b, i: (b, i, 0)),
        pl.BlockSpec((1, N_MOD, D_MODEL), lambda b, i: (b, 0, 0)),
        pl.BlockSpec((1, 1, D_MODEL), lambda b, i: (3 * layer + sub, 0, 0)),
        pl.BlockSpec((1, D_MODEL, 2 * D_FF), lambda b, i: (layer, 0, 0), pipeline_mode=resident),
        pl.BlockSpec((1, D_FF, D_MODEL), lambda b, i: (layer, 0, 0), pipeline_mode=resident),
    ]
    operands = [x, mod, g_norm, w_in, w_out]
    if mix is not None:
        in_specs += [
            pl.BlockSpec((1, tm, Q_W), lambda b, i: (b, i, 0)),
            pl.BlockSpec((1, tm, Q_W), lambda b, i: (b, i, 0)),
            pl.BlockSpec((1, 2 * Q_W, D_MODEL), lambda b, i: (layer, 0, 0), pipeline_mode=resident),
        ]
        operands += list(mix)
    return pl.pallas_call(
        functools.partial(_ffn_kernel, rows=rows, mix_gate_row=None if mix is None else 3 * sub - 1),
        grid=(batch, seq // tm),
        in_specs=in_specs,
        out_specs=pl.BlockSpec((1, tm, D_MODEL), lambda b, i: (b, i, 0)),
        out_shape=jax.ShapeDtypeStruct(x.shape, F32),
        compiler_params=_params("parallel", "parallel"),
        name="ffn",
    )(*operands)


def _head_sumsq(t, ones_bd):
    return _dot((t * t).astype(BF16), ones_bd)


def _rope(t, cos, sin, first_half):
    up = pltpu.roll(t, LANES - HEAD_DIM // 4, axis=1)
    dn = pltpu.roll(t, HEAD_DIM // 4, axis=1)
    return t * cos + jnp.where(first_half, up, dn) * sin


def _store_vx(vx_ref, v_t):
    ones = jnp.ones((VX_ROWS - HEAD_DIM, v_t.shape[1]), F32)
    for kv in range(KV_HEADS):
        vx_ref[0, kv] = jnp.concatenate(
            [v_t[kv * HEAD_DIM:(kv + 1) * HEAD_DIM], ones], axis=0).astype(BF16)


def _qkv_kernel(x_ref, mod_ref, g_ref, w_ref, gain_ref, cos_ref, sin_ref, ones_ref,
                qta_ref, ka_ref, vxa_ref, qtb_ref, kb_ref, vxb_ref):
    tm = x_ref.shape[1]
    h = _ada_norm(x_ref[0], g_ref[0], mod_ref[0, 4:5, :], mod_ref[0, 3:4, :]).astype(BF16)
    qkv = _dot(h, w_ref[0])
    ones_bd = ones_ref[...]
    gain = gain_ref[0]

    normed = []
    for s in range(QKV_W // MXU_DIM):
        t = qkv[:, s * MXU_DIM:(s + 1) * MXU_DIM]
        ss = _head_sumsq(t, ones_bd)
        normed.append(t * lax.rsqrt(ss * (1.0 / HEAD_DIM) + EPS)
                      * gain[:, s * MXU_DIM:(s + 1) * MXU_DIM])
    normed = jnp.concatenate(normed, axis=1)

    cos = cos_ref[...]
    sin = sin_ref[...]
    lane = lax.broadcasted_iota(jnp.int32, (tm, LANES), 1)
    first_half = (lane % (HEAD_DIM // 2)) < (HEAD_DIM // 4)

    qa = [_rope(normed[:, s * LANES:(s + 1) * LANES], cos, sin, first_half) * (SM_SCALE * LOG2E)
          for s in range(Q_W // LANES)]
    qa = jnp.concatenate(qa, axis=1)
    qta_ref[0] = qa.T.astype(BF16)
    ka_ref[0] = _rope(normed[:, Q_W:Q_W + KV_W], cos, sin, first_half).astype(BF16)
    _store_vx(vxa_ref, qkv[:, Q_W + KV_W:Q_W + 2 * KV_W].T)

    off = Q_W + 2 * KV_W
    qtb_ref[0] = (normed[:, off:off + Q_W] * (SM_SCALE * LOG2E)).T.astype(BF16)
    kb_ref[0] = normed[:, off + Q_W:off + Q_W + KV_W].astype(BF16)
    _store_vx(vxb_ref, qkv[:, off + Q_W + KV_W:off + Q_W + 2 * KV_W].T)


def _qkv(x, mod, g_norm, w_qkv, gains, cos, sin, ones_bd, layer, tm):
    batch, seq, _ = x.shape
    return pl.pallas_call(
        _qkv_kernel,
        grid=(batch, seq // tm),
        in_specs=[
            pl.BlockSpec((1, tm, D_MODEL), lambda b, i: (b, i, 0)),
            pl.BlockSpec((1, N_MOD, D_MODEL), lambda b, i: (b, 0, 0)),
            pl.BlockSpec((1, 1, D_MODEL), lambda b, i: (3 * layer + 1, 0, 0)),
            pl.BlockSpec((1, D_MODEL, QKV_W), lambda b, i: (layer, 0, 0)),
            pl.BlockSpec((1, 1, QKV_W), lambda b, i: (layer, 0, 0)),
            pl.BlockSpec((tm, LANES), lambda b, i: (i, 0)),
            pl.BlockSpec((tm, LANES), lambda b, i: (i, 0)),
            pl.BlockSpec((MXU_DIM, MXU_DIM), lambda b, i: (0, 0)),
        ],
        out_specs=2 * [
            pl.BlockSpec((1, Q_W, tm), lambda b, i: (b, 0, i)),
            pl.BlockSpec((1, tm, KV_W), lambda b, i: (b, i, 0)),
            pl.BlockSpec((1, KV_HEADS, VX_ROWS, tm), lambda b, i: (b, 0, 0, i)),
        ],
        out_shape=2 * [
            jax.ShapeDtypeStruct((batch, Q_W, seq), BF16),
            jax.ShapeDtypeStruct((batch, seq, KV_W), BF16),
            jax.ShapeDtypeStruct((batch, KV_HEADS, VX_ROWS, seq), BF16),
        ],
        compiler_params=_params("parallel", "parallel"),
        name="qkv",
    )(x, mod, g_norm, w_qkv, gains, cos, sin, ones_bd)


def _attn_a_kernel(qt_ref, k_ref, vx_ref, o_ref, s_scr, *, seq, tk, tiles_per_trip):
    kv = pl.program_id(1)
    tq = qt_ref.shape[2]
    nq = GROUP * tq
    n_tiles = seq // tk
    q4 = qt_ref[0]
    qt = jnp.concatenate([q4[g * HEAD_DIM:(g + 1) * HEAD_DIM] for g in range(GROUP)], axis=1)
    zero = jnp.zeros_like(qt)
    q_pad = jnp.concatenate([jnp.where(kv == h, qt, zero) for h in range(KV_HEADS)], axis=0)

    def scores(j, slot, m):
        off = pl.multiple_of(jnp.minimum(j, n_tiles - 1) * tk, tk)
        s_t = _dot(k_ref[0, pl.ds(off, tk), :], q_pad)
        s_scr[slot] = s_t
        return jnp.maximum(m, jnp.max(s_t, axis=0, keepdims=True))

    def accumulate(j, slot, m_before, m_now, acc):
        off = pl.multiple_of(j * tk, tk)
        alpha = jnp.exp2(m_before - m_now)
        p_t = jnp.exp2(s_scr[slot] - m_now).astype(BF16)
        return alpha * acc + _dot(vx_ref[0, 0, :, pl.ds(off, tk)], p_t)

    def body(trip, carry):
        m_prev, m_cur, acc = carry
        j = tiles_per_trip * trip
        for u in range(tiles_per_trip):
            m_next = scores(j + u + 1, (u + 1) % 2, m_cur)
            acc = accumulate(j + u, u % 2, m_prev, m_cur, acc)
            m_prev, m_cur = m_cur, m_next
        return m_prev, m_cur, acc

    m0 = jnp.full((1, nq), -jnp.inf, F32)
    m1 = scores(0, 0, m0)
    acc0 = jnp.zeros((VX_ROWS, nq), F32)
    _, _, acc = lax.fori_loop(0, n_tiles // tiles_per_trip, body, (m0, m1, acc0))
    out_t = acc[:HEAD_DIM] / acc[HEAD_DIM:HEAD_DIM + 1]
    out_t = jnp.concatenate([out_t[:, g * tq:(g + 1) * tq] for g in range(GROUP)], axis=0)
    o_ref[0] = out_t.T.astype(BF16)


def _attn_a(qt_a, k_a, vx_a, tq, tk, tiles_per_trip):
    batch, _, seq = qt_a.shape
    gw = GROUP * HEAD_DIM
    assert tiles_per_trip % 2 == 0 and (seq // tk) % tiles_per_trip == 0
    return pl.pallas_call(
        functools.partial(_attn_a_kernel, seq=seq, tk=tk, tiles_per_trip=tiles_per_trip),
        grid=(batch, KV_HEADS, seq // tq),
        in_specs=[
            pl.BlockSpec((1, gw, tq), lambda b, h, i: (b, h, i)),
            pl.BlockSpec((1, seq, KV_W), lambda b, h, i: (b, 0, 0)),
            pl.BlockSpec((1, 1, VX_ROWS, seq), lambda b, h, i: (b, h, 0, 0)),
        ],
        out_specs=pl.BlockSpec((1, tq, gw), lambda b, h, i: (b, i, h)),
        out_shape=jax.ShapeDtypeStruct((batch, seq, Q_W), BF16),
        scratch_shapes=[pltpu.VMEM((2, tk, GROUP * tq), F32)],
        compiler_params=_params("parallel", "parallel", "parallel"),
        name="attn_global",
    )(qt_a, k_a, vx_a)


def _attn_b_kernel(sink_ref, qt_ref, kp_ref, kc_ref, kn_ref, vp_ref, vc_ref, vn_ref, bias_ref, o_ref, s_scr,
                   *, seq, layer, blocks):
    first_block = pl.program_id(1) * blocks
    n_blocks = seq // Q_BLOCK
    gw = GROUP * HEAD_DIM
    nq = GROUP * Q_BLOCK
    k_all = jnp.concatenate([kp_ref[0], kc_ref[0], kn_ref[0]], axis=0)
    lane_head = lax.broadcasted_iota(jnp.int32, (1, nq), 1) // Q_BLOCK
    zero = jnp.zeros((HEAD_DIM, nq), BF16)
    sinks = []
    for kv in range(KV_HEADS):
        sink = jnp.zeros((1, nq), F32)
        for g in range(GROUP):
            sink = jnp.where(lane_head == g, sink_ref[layer, kv * GROUP + g] * LOG2E, sink)
        sinks.append(sink)

    def scores(kv, blk, slot):
        lo = blk * Q_BLOCK
        q4 = qt_ref[0, kv * gw:(kv + 1) * gw, lo:lo + Q_BLOCK]
        qt = jnp.concatenate([q4[g * HEAD_DIM:(g + 1) * HEAD_DIM] for g in range(GROUP)], axis=1)
        q_pad = jnp.concatenate([qt, zero] if kv == 0 else [zero, qt], axis=0)
        s_t = _dot(k_all[lo:lo + BAND], q_pad) + bias_ref[kv]
        parts = [s_t[:Q_BLOCK], s_t[Q_BLOCK:2 * Q_BLOCK], s_t[2 * Q_BLOCK:]]
        if blk == 0:
            parts[0] = jnp.where(first_block > 0, parts[0], -jnp.inf)
        if blk == blocks - 1:
            parts[2] = jnp.where(first_block + blocks < n_blocks, parts[2], -jnp.inf)
        s_t = jnp.concatenate(parts, axis=0)
        s_scr[slot] = s_t
        return jnp.maximum(jnp.max(s_t, axis=0, keepdims=True), sinks[kv])

    def finish(kv, blk, slot, m):
        lo = blk * Q_BLOCK
        v_band = jnp.concatenate([vp_ref[0, kv], vc_ref[0, kv], vn_ref[0, kv]], axis=1)[:, lo:lo + BAND]
        p_t = jnp.exp2(s_scr[slot] - m).astype(BF16)
        acc = _dot(v_band, p_t)
        denom = acc[HEAD_DIM:HEAD_DIM + 1] + jnp.exp2(sinks[kv] - m)
        out_t = acc[:HEAD_DIM] / denom
        out_t = jnp.concatenate([out_t[:, g * Q_BLOCK:(g + 1) * Q_BLOCK] for g in range(GROUP)], axis=0)
        o_ref[0, lo:lo + Q_BLOCK, kv * gw:(kv + 1) * gw] = out_t.T.astype(BF16)

    chains = [(kv, blk) for kv in range(KV_HEADS) for blk in range(blocks)]
    m_next = scores(*chains[0], 0)
    for c, chain in enumerate(chains):
        m = m_next
        if c + 1 < len(chains):
            m_next = scores(*chains[c + 1], (c + 1) % 2)
        finish(*chain, c % 2, m)


def _attn_b(qt_b, k_b, vx_b, band_bias, sinks, layer, blocks):
    batch, _, seq = qt_b.shape
    nb = seq // Q_BLOCK
    tq = blocks * Q_BLOCK
    prev_blk = lambda i: jnp.maximum(i * blocks - 1, 0)
    next_blk = lambda i: jnp.minimum((i + 1) * blocks, nb - 1)
    return pl.pallas_call(
        functools.partial(_attn_b_kernel, seq=seq, layer=layer, blocks=blocks),
        grid=(batch, seq // tq),
        in_specs=[
            pl.BlockSpec(memory_space=pltpu.SMEM),
            pl.BlockSpec((1, Q_W, tq), lambda b, i: (b, 0, i)),
            pl.BlockSpec((1, Q_BLOCK, KV_W), lambda b, i: (b, prev_blk(i), 0)),
            pl.BlockSpec((1, tq, KV_W), lambda b, i: (b, i, 0)),
            pl.BlockSpec((1, Q_BLOCK, KV_W), lambda b, i: (b, next_blk(i), 0)),
            pl.BlockSpec((1, KV_HEADS, VX_ROWS, Q_BLOCK), lambda b, i: (b, 0, 0, prev_blk(i))),
            pl.BlockSpec((1, KV_HEADS, VX_ROWS, tq), lambda b, i: (b, 0, 0, i)),
            pl.BlockSpec((1, KV_HEADS, VX_ROWS, Q_BLOCK), lambda b, i: (b, 0, 0, next_blk(i))),
            pl.BlockSpec((KV_HEADS, BAND, GROUP * Q_BLOCK), lambda b, i: (0, 0, 0)),
        ],
        out_specs=pl.BlockSpec((1, tq, Q_W), lambda b, i: (b, i, 0)),
        out_shape=jax.ShapeDtypeStruct((batch, seq, Q_W), BF16),
        scratch_shapes=[pltpu.VMEM((2, BAND, GROUP * Q_BLOCK), F32)],
        compiler_params=_params("parallel", "parallel"),
        name="attn_window",
    )(sinks, qt_b, k_b, k_b, k_b, vx_b, vx_b, vx_b, band_bias)


def _rope_tables(seq):
    pos = jnp.arange(seq)
    row = (pos // GRID_W).astype(F32)
    col = (pos % GRID_W).astype(F32)
    axis_dim = HEAD_DIM // 2
    freqs = ROPE_THETA ** (-jnp.arange(0, axis_dim, 2, dtype=F32) / axis_dim)
    ang_r = row[:, None] * freqs[None, :]
    ang_c = col[:, None] * freqs[None, :]
    cr, sr, cc, sc = jnp.cos(ang_r), jnp.sin(ang_r), jnp.cos(ang_c), jnp.sin(ang_c)
    cos = jnp.concatenate([cr, cr, cc, cc], axis=1)
    sin = jnp.concatenate([-sr, sr, -sc, sc], axis=1)
    reps = LANES // HEAD_DIM
    return jnp.tile(cos, (1, reps)), jnp.tile(sin, (1, reps))


def _qkv_gains(g_qk):
    ones = jnp.ones((DEPTH, KV_W), F32)
    t = lambda i, n: jnp.tile(g_qk[:, i, :].astype(F32), (1, n))
    return jnp.concatenate([t(0, Q_HEADS), t(1, KV_HEADS), ones,
                            t(2, Q_HEADS), t(3, KV_HEADS), ones], axis=1)[:, None, :]


def _tiles(seq):
    tm_ffn = min(512, seq)
    tm_qkv = min(512, seq)
    tq = min(2 * Q_BLOCK, seq)
    tk = min(MXU_DIM, seq)
    tiles_per_trip = min(32, seq // tk)
    win_blocks = min(8, seq // Q_BLOCK)
    return tm_ffn, tm_qkv, tq, tk, tiles_per_trip, win_blocks


def _trunk(x, mod_all, weights, band_bias):
    seq = x.shape[1]
    tm_ffn, tm_qkv, tq, tk, tiles_per_trip, win_blocks = _tiles(seq)
    cos, sin = _rope_tables(seq)
    for l in range(DEPTH):
        mod = mod_all[l]
        x = _ffn(x, mod, weights["g_norm"], weights["ffn1_w_in"], weights["ffn1_w_out"], l, 0, tm_ffn)
        qt_a, k_a, vx_a, qt_b, k_b, vx_b = _qkv(x, mod, weights["g_norm"], weights["w_qkv"], weights["gains"],
                                                cos, sin, weights["ones_bd"], l, tm_qkv)
        out_a = _attn_a(qt_a, k_a, vx_a, tq, tk, tiles_per_trip)
        out_b = _attn_b(qt_b, k_b, vx_b, band_bias, weights["sinks"], l, win_blocks)
        x = _ffn(x, mod, weights["g_norm"], weights["ffn2_w_in"], weights["ffn2_w_out"], l, 2, tm_ffn,
                 mix=(out_a, out_b, weights["w_o"]))
    return x


def _prepare(w_ada, b_ada, g_norm, ffn1_w_in, ffn1_w_out, w_qkv, g_qk, attn_sinks, w_o,
             ffn2_w_in, ffn2_w_out):
    head_id = np.arange(MXU_DIM) // HEAD_DIM
    return {
        "g_norm": g_norm.astype(F32).reshape(DEPTH * 3, 1, D_MODEL),
        "ffn1_w_in": ffn1_w_in.astype(BF16), "ffn1_w_out": ffn1_w_out.astype(BF16),
        "ffn2_w_in": ffn2_w_in.astype(BF16), "ffn2_w_out": ffn2_w_out.astype(BF16),
        "w_qkv": w_qkv.astype(BF16), "w_o": w_o.astype(BF16),
        "gains": _qkv_gains(g_qk),
        "sinks": attn_sinks.astype(F32),
        "ones_bd": jnp.asarray(head_id[:, None] == head_id[None, :], BF16),
    }


def kernel(x_prompt, x_sample, c_prompt, c_sample, w_ada, b_ada, g_norm, ffn1_w_in, ffn1_w_out,
           w_qkv, g_qk, attn_sinks, rel_bias, w_o, ffn2_w_in, ffn2_w_out):
    weights = _prepare(w_ada, b_ada, g_norm, ffn1_w_in, ffn1_w_out, w_qkv, g_qk, attn_sinks, w_o,
                       ffn2_w_in, ffn2_w_out)
    nb_p, nb_s = c_prompt.shape[0], c_sample.shape[0]
    c_all = jnp.concatenate(
        [c_prompt, c_sample, jnp.zeros((C_ROWS - nb_p - nb_s, D_MODEL), F32)], axis=0)
    mod = _modulation(c_all, w_ada, b_ada).reshape(DEPTH, C_ROWS, N_MOD, D_MODEL)
    band_bias = _band_bias(rel_bias)
    y_prompt = _trunk(x_prompt, mod[:, :nb_p], weights, band_bias)
    y_sample = _trunk(x_sample, mod[:, nb_p:nb_p + nb_s], weights, band_bias)
    return (y_prompt, y_sample)
```

```python
import functools
import math

import jax
import jax.numpy as jnp
import numpy as np
from jax import lax
from jax.experimental import pallas as pl
from jax.experimental.pallas import tpu as pltpu

F32 = jnp.float32
BF16 = jnp.bfloat16

D_MODEL = 1024
DEPTH = 4
HEAD_DIM = 64
Q_HEADS = 8
KV_HEADS = 2
GROUP = Q_HEADS // KV_HEADS
Q_W = Q_HEADS * HEAD_DIM
KV_W = KV_HEADS * HEAD_DIM
QKV_W = 2 * (Q_W + 2 * KV_W)
D_FF = 2816
GRID_W = 64
Q_BLOCK = 128
WINDOW = 128
BAND = Q_BLOCK + 2 * WINDOW
NUM_BUCKETS = 32
MAX_DISTANCE = 128
ROPE_THETA = 10000.0
EPS = 1e-6
N_MOD = 9
SM_SCALE = HEAD_DIM ** -0.5
LOG2E = math.log2(math.e)
BF16_SUBLANES = 16
VX_ROWS = HEAD_DIM + BF16_SUBLANES

LANES = 128
MXU_DIM = 256
VMEM_LIMIT = 56 * 1024 * 1024

C_ROWS = 16


def _dot(a, b):
    return jnp.dot(a, b, preferred_element_type=F32)


def _silu(a):
    return a / (1.0 + jnp.exp(-a))


def _params(*sem):
    return pltpu.CompilerParams(dimension_semantics=sem, vmem_limit_bytes=VMEM_LIMIT)


def _mod_kernel(c_ref, w_ref, b_ref, o_ref):
    c = c_ref[...]
    o_ref[0] = _dot(_silu(c).astype(BF16), w_ref[0].astype(BF16)) + b_ref[0]


def _modulation(c_all, w_ada, b_ada):
    tn = D_MODEL
    n_out = N_MOD * D_MODEL
    return pl.pallas_call(
        _mod_kernel,
        grid=(DEPTH, n_out // tn),
        in_specs=[
            pl.BlockSpec((C_ROWS, D_MODEL), lambda l, j: (0, 0)),
            pl.BlockSpec((1, D_MODEL, tn), lambda l, j: (l, 0, j)),
            pl.BlockSpec((1, 1, tn), lambda l, j: (l, 0, j)),
        ],
        out_specs=pl.BlockSpec((1, C_ROWS, tn), lambda l, j: (l, 0, j)),
        out_shape=jax.ShapeDtypeStruct((DEPTH, C_ROWS, n_out), F32),
        compiler_params=_params("parallel", "parallel"),
        name="adaln_mod",
    )(c_all, w_ada, b_ada.reshape(DEPTH, 1, n_out))


def _bias_kernel(rb_ref, bucket_ref, o_ref):
    bucket = bucket_ref[...]
    for h in range(Q_HEADS):
        acc = jnp.full(bucket.shape, -jnp.inf, F32)
        for b in range(NUM_BUCKETS):
            acc = jnp.where(bucket == b, rb_ref[b, h] * LOG2E, acc)
        g = h % GROUP
        o_ref[h // GROUP, :, g * Q_BLOCK:(g + 1) * Q_BLOCK] = acc


def _t5_bucket(rel):
    half = NUM_BUCKETS // 2
    max_exact = half // 2
    base = jnp.where(rel > 0, half, 0)
    n = jnp.abs(rel)
    nf = jnp.maximum(n, 1).astype(F32)
    large = max_exact + (jnp.log(nf / max_exact) / math.log(MAX_DISTANCE / max_exact)
                         * (half - max_exact)).astype(jnp.int32)
    large = jnp.minimum(large, half - 1)
    return (base + jnp.where(n < max_exact, n, large)).astype(jnp.int32)


def _band_bias(rel_bias):
    kk = jnp.arange(BAND)[:, None]
    qq = jnp.arange(Q_BLOCK)[None, :]
    rel = kk - WINDOW - qq
    bucket = jnp.where(jnp.abs(rel) <= WINDOW, _t5_bucket(rel), -1)
    return pl.pallas_call(
        _bias_kernel,
        in_specs=[
            pl.BlockSpec(memory_space=pltpu.SMEM),
            pl.BlockSpec((BAND, Q_BLOCK), lambda: (0, 0)),
        ],
        out_specs=pl.BlockSpec((KV_HEADS, BAND, GROUP * Q_BLOCK), lambda: (0, 0, 0)),
        out_shape=jax.ShapeDtypeStruct((KV_HEADS, BAND, GROUP * Q_BLOCK), F32),
        name="band_bias",
    )(rel_bias.astype(F32), bucket)


def _ada_norm(x, g, scale, shift):
    y = x * lax.rsqrt(jnp.mean(x * x, axis=-1, keepdims=True) + EPS)
    return y * (g * (1.0 + scale)) + shift


def _ffn_kernel(x_ref, mod_ref, g_ref, wi_ref, wo_ref, *rest, rows, mix_gate_row):
    sh_i, sc_i, gt_i = rows
    x = x_ref[0]
    if mix_gate_row is None:
        (o_ref,) = rest
    else:
        oa_ref, ob_ref, wmix_ref, o_ref = rest
        mix = jnp.concatenate([oa_ref[0], ob_ref[0]], axis=1)
        x = x + mod_ref[0, mix_gate_row:mix_gate_row + 1, :] * _dot(mix, wmix_ref[0])
    h = _ada_norm(x, g_ref[0], mod_ref[0, sc_i:sc_i + 1, :], mod_ref[0, sh_i:sh_i + 1, :]).astype(BF16)
    acc = jnp.zeros(x.shape, F32)
    for c in range(D_FF // MXU_DIM):
        lo = c * MXU_DIM
        a = _dot(h, wi_ref[0, :, lo:lo + MXU_DIM])
        b = _dot(h, wi_ref[0, :, D_FF + lo:D_FF + lo + MXU_DIM])
        acc = acc + _dot((_silu(a) * b).astype(BF16), wo_ref[0, lo:lo + MXU_DIM, :])
    o_ref[0] = x + 0.5 * mod_ref[0, gt_i:gt_i + 1, :] * acc


def _ffn(x, mod, g_norm, w_in, w_out, layer, sub, tm, mix=None):
    batch, seq, _ = x.shape
    rows = (3 * sub, 3 * sub + 1, 3 * sub + 2)
    resident = pl.Buffered(1)
    in_specs = [
        pl.BlockSpec((1, tm, D_MODEL), lambda b, i: (b, i, 0)),
        pl.BlockSpec((1, N_MOD, D_MODEL), lambda b, i: (b, 0, 0)),
        pl.BlockSpec((1, 1, D_MODEL), lambda b, i: (3 * layer + sub, 0, 0)),
        pl.BlockSpec((1, D_MODEL, 2 * D_FF), lambda b, i: (layer, 0, 0), pipeline_mode=resident),
        pl.BlockSpec((1, D_FF, D_MODEL), lambda b, i: (layer, 0, 0), pipeline_mode=resident),
    ]
    operands = [x, mod, g_norm, w_in, w_out]
    if mix is not None:
        in_specs += [
            pl.BlockSpec((1, tm, Q_W), lambda b, i: (b, i, 0)),
            pl.BlockSpec((1, tm, Q_W), lambda b, i: (b, i, 0)),
            pl.BlockSpec((1, 2 * Q_W, D_MODEL), lambda b, i: (layer, 0, 0), pipeline_mode=resident),
        ]
        operands += list(mix)
    return pl.pallas_call(
        functools.partial(_ffn_kernel, rows=rows, mix_gate_row=None if mix is None else 3 * sub - 1),
        grid=(batch, seq // tm),
        in_specs=in_specs,
        out_specs=pl.BlockSpec((1, tm, D_MODEL), lambda b, i: (b, i, 0)),
        out_shape=jax.ShapeDtypeStruct(x.shape, F32),
        compiler_params=_params("parallel", "parallel"),
        name="ffn",
    )(*operands)


def _head_sumsq(t, ones_bd):
    return _dot((t * t).astype(BF16), ones_bd)


def _rope(t, cos, sin, first_half):
    up = pltpu.roll(t, LANES - HEAD_DIM // 4, axis=1)
    dn = pltpu.roll(t, HEAD_DIM // 4, axis=1)
    return t * cos + jnp.where(first_half, up, dn) * sin


def _store_vx(vx_ref, cols, v_t):
    ones = jnp.ones((VX_ROWS - HEAD_DIM, v_t.shape[1]), F32)
    for kv in range(KV_HEADS):
        vx_ref[0, kv, :, cols] = jnp.concatenate(
            [v_t[kv * HEAD_DIM:(kv + 1) * HEAD_DIM], ones], axis=0).astype(BF16)


def _qkv_kernel(x_ref, mod_ref, g_ref, w_ref, gain_ref, cos_ref, sin_ref, ones_ref,
                qta_ref, ka_ref, vxa_ref, qtb_ref, kb_ref, vxb_ref, *, rows_per_chunk):
    tm = x_ref.shape[1]
    rc = rows_per_chunk
    ones_bd = ones_ref[...]
    gain = gain_ref[0]
    lane = lax.broadcasted_iota(jnp.int32, (rc, LANES), 1)
    first_half = (lane % (HEAD_DIM // 2)) < (HEAD_DIM // 4)

    def project(r0):
        h = _ada_norm(x_ref[0, r0:r0 + rc], g_ref[0], mod_ref[0, 4:5, :], mod_ref[0, 3:4, :])
        return _dot(h.astype(BF16), w_ref[0])

    def finish(r0, qkv):
        rows = slice(r0, r0 + rc)
        normed = []
        for s in range(QKV_W // MXU_DIM):
            t = qkv[:, s * MXU_DIM:(s + 1) * MXU_DIM]
            ss = _head_sumsq(t, ones_bd)
            normed.append(t * (lax.rsqrt(ss * (1.0 / HEAD_DIM) + EPS) * gain[:, s * MXU_DIM:(s + 1) * MXU_DIM]))
        normed = jnp.concatenate(normed, axis=1)
        cos = cos_ref[rows]
        sin = sin_ref[rows]
        qa = jnp.concatenate([_rope(normed[:, s * LANES:(s + 1) * LANES], cos, sin, first_half)
                              for s in range(Q_W // LANES)], axis=1)
        qta_ref[0, :, rows] = qa.T.astype(BF16)
        ka_ref[0, rows] = _rope(normed[:, Q_W:Q_W + KV_W], cos, sin, first_half).astype(BF16)
        _store_vx(vxa_ref, rows, qkv[:, Q_W + KV_W:Q_W + 2 * KV_W].T)
        off = Q_W + 2 * KV_W
        qtb_ref[0, :, rows] = normed[:, off:off + Q_W].T.astype(BF16)
        kb_ref[0, rows] = normed[:, off + Q_W:off + Q_W + KV_W].astype(BF16)
        _store_vx(vxb_ref, rows, qkv[:, off + Q_W + KV_W:off + Q_W + 2 * KV_W].T)

    starts = list(range(0, tm, rc))
    qkv_next = project(starts[0])
    for c, r0 in enumerate(starts):
        qkv = qkv_next
        if c + 1 < len(starts):
            qkv_next = project(starts[c + 1])
        finish(r0, qkv)


def _qkv(x, mod, g_norm, w_qkv, gains, cos, sin, ones_bd, layer, tm):
    batch, seq, _ = x.shape
    return pl.pallas_call(
        functools.partial(_qkv_kernel, rows_per_chunk=min(MXU_DIM, tm)),
        grid=(batch, seq // tm),
        in_specs=[
            pl.BlockSpec((1, tm, D_MODEL), lambda b, i: (b, i, 0)),
            pl.BlockSpec((1, N_MOD, D_MODEL), lambda b, i: (b, 0, 0)),
            pl.BlockSpec((1, 1, D_MODEL), lambda b, i: (3 * layer + 1, 0, 0)),
            pl.BlockSpec((1, D_MODEL, QKV_W), lambda b, i: (layer, 0, 0)),
            pl.BlockSpec((1, 1, QKV_W), lambda b, i: (layer, 0, 0)),
            pl.BlockSpec((tm, LANES), lambda b, i: (i, 0)),
            pl.BlockSpec((tm, LANES), lambda b, i: (i, 0)),
            pl.BlockSpec((MXU_DIM, MXU_DIM), lambda b, i: (0, 0)),
        ],
        out_specs=2 * [
            pl.BlockSpec((1, Q_W, tm), lambda b, i: (b, 0, i)),
            pl.BlockSpec((1, tm, KV_W), lambda b, i: (b, i, 0)),
            pl.BlockSpec((1, KV_HEADS, VX_ROWS, tm), lambda b, i: (b, 0, 0, i)),
        ],
        out_shape=2 * [
            jax.ShapeDtypeStruct((batch, Q_W, seq), BF16),
            jax.ShapeDtypeStruct((batch, seq, KV_W), BF16),
            jax.ShapeDtypeStruct((batch, KV_HEADS, VX_ROWS, seq), BF16),
        ],
        compiler_params=_params("parallel", "parallel"),
        name="qkv",
    )(x, mod, g_norm, w_qkv, gains, cos, sin, ones_bd)


def _attn_a_kernel(qt_ref, k_ref, vx_ref, o_ref, s_scr, *, seq, tk, tiles_per_trip):
    kv = pl.program_id(1)
    tq = qt_ref.shape[2]
    nq = GROUP * tq
    n_tiles = seq // tk
    q4 = qt_ref[0]
    qt = jnp.concatenate([q4[g * HEAD_DIM:(g + 1) * HEAD_DIM] for g in range(GROUP)], axis=1)
    zero = jnp.zeros_like(qt)
    q_pad = jnp.concatenate([jnp.where(kv == h, qt, zero) for h in range(KV_HEADS)], axis=0)

    def scores(j, slot, m):
        off = pl.multiple_of(jnp.minimum(j, n_tiles - 1) * tk, tk)
        s_t = _dot(k_ref[0, pl.ds(off, tk), :], q_pad)
        s_scr[slot] = s_t
        return jnp.maximum(m, jnp.max(s_t, axis=0, keepdims=True))

    def accumulate(j, slot, m_before, m_now, acc):
        off = pl.multiple_of(j * tk, tk)
        alpha = jnp.exp2(m_before - m_now)
        p_t = jnp.exp2(s_scr[slot] - m_now).astype(BF16)
        return alpha * acc + _dot(vx_ref[0, 0, :, pl.ds(off, tk)], p_t)

    def body(trip, carry):
        m_prev, m_cur, acc = carry
        j = tiles_per_trip * trip
        for u in range(tiles_per_trip):
            m_next = scores(j + u + 1, (u + 1) % 2, m_cur)
            acc = accumulate(j + u, u % 2, m_prev, m_cur, acc)
            m_prev, m_cur = m_cur, m_next
        return m_prev, m_cur, acc

    m0 = jnp.full((1, nq), -jnp.inf, F32)
    m1 = scores(0, 0, m0)
    acc0 = jnp.zeros((VX_ROWS, nq), F32)
    _, _, acc = lax.fori_loop(0, n_tiles // tiles_per_trip, body, (m0, m1, acc0))
    out_t = acc[:HEAD_DIM] / acc[HEAD_DIM:HEAD_DIM + 1]
    out_t = jnp.concatenate([out_t[:, g * tq:(g + 1) * tq] for g in range(GROUP)], axis=0)
    o_ref[0] = out_t.T.astype(BF16)


def _attn_a(qt_a, k_a, vx_a, tq, tk, tiles_per_trip):
    batch, _, seq = qt_a.shape
    gw = GROUP * HEAD_DIM
    assert tiles_per_trip % 2 == 0 and (seq // tk) % tiles_per_trip == 0
    return pl.pallas_call(
        functools.partial(_attn_a_kernel, seq=seq, tk=tk, tiles_per_trip=tiles_per_trip),
        grid=(batch, KV_HEADS, seq // tq),
        in_specs=[
            pl.BlockSpec((1, gw, tq), lambda b, h, i: (b, h, i)),
            pl.BlockSpec((1, seq, KV_W), lambda b, h, i: (b, 0, 0)),
            pl.BlockSpec((1, 1, VX_ROWS, seq), lambda b, h, i: (b, h, 0, 0)),
        ],
        out_specs=pl.BlockSpec((1, tq, gw), lambda b, h, i: (b, i, h)),
        out_shape=jax.ShapeDtypeStruct((batch, seq, Q_W), BF16),
        scratch_shapes=[pltpu.VMEM((2, tk, GROUP * tq), F32)],
        compiler_params=_params("parallel", "parallel", "parallel"),
        name="attn_global",
    )(qt_a, k_a, vx_a)


def _attn_b_kernel(sink_ref, qt_ref, kp_ref, kc_ref, kn_ref, vp_ref, vc_ref, vn_ref, bias_ref, o_ref, s_scr,
                   *, seq, layer, blocks):
    first_block = pl.program_id(1) * blocks
    n_blocks = seq // Q_BLOCK
    gw = GROUP * HEAD_DIM
    nq = GROUP * Q_BLOCK
    k_all = jnp.concatenate([kp_ref[0], kc_ref[0], kn_ref[0]], axis=0)
    lane_head = lax.broadcasted_iota(jnp.int32, (1, nq), 1) // Q_BLOCK
    zero = jnp.zeros((HEAD_DIM, nq), BF16)
    sinks = []
    for kv in range(KV_HEADS):
        sink = jnp.zeros((1, nq), F32)
        for g in range(GROUP):
            sink = jnp.where(lane_head == g, sink_ref[layer, kv * GROUP + g] * LOG2E, sink)
        sinks.append(sink)

    def scores(kv, blk, slot):
        lo = blk * Q_BLOCK
        q4 = qt_ref[0, kv * gw:(kv + 1) * gw, lo:lo + Q_BLOCK]
        qt = jnp.concatenate([q4[g * HEAD_DIM:(g + 1) * HEAD_DIM] for g in range(GROUP)], axis=1)
        q_pad = jnp.concatenate([qt, zero] if kv == 0 else [zero, qt], axis=0)
        s_t = _dot(k_all[lo:lo + BAND], q_pad) + bias_ref[kv]
        parts = [s_t[:Q_BLOCK], s_t[Q_BLOCK:2 * Q_BLOCK], s_t[2 * Q_BLOCK:]]
        if blk == 0:
            parts[0] = jnp.where(first_block > 0, parts[0], -jnp.inf)
        if blk == blocks - 1:
            parts[2] = jnp.where(first_block + blocks < n_blocks, parts[2], -jnp.inf)
        s_t = jnp.concatenate(parts, axis=0)
        s_scr[slot] = s_t
        return jnp.maximum(jnp.max(s_t, axis=0, keepdims=True), sinks[kv])

    def finish(kv, blk, slot, m):
        lo = blk * Q_BLOCK
        v_band = jnp.concatenate([vp_ref[0, kv], vc_ref[0, kv], vn_ref[0, kv]], axis=1)[:, lo:lo + BAND]
        p_t = jnp.exp2(s_scr[slot] - m).astype(BF16)
        acc = _dot(v_band, p_t)
        denom = acc[HEAD_DIM:HEAD_DIM + 1] + jnp.exp2(sinks[kv] - m)
        out_t = acc[:HEAD_DIM] / denom
        out_t = jnp.concatenate([out_t[:, g * Q_BLOCK:(g + 1) * Q_BLOCK] for g in range(GROUP)], axis=0)
        o_ref[0, lo:lo + Q_BLOCK, kv * gw:(kv + 1) * gw] = out_t.T.astype(BF16)

    chains = [(kv, blk) for kv in range(KV_HEADS) for blk in range(blocks)]
    m_next = scores(*chains[0], 0)
    for c, chain in enumerate(chains):
        m = m_next
        if c + 1 < len(chains):
            m_next = scores(*chains[c + 1], (c + 1) % 2)
        finish(*chain, c % 2, m)


def _attn_b(qt_b, k_b, vx_b, band_bias, sinks, layer, blocks):
    batch, _, seq = qt_b.shape
    nb = seq // Q_BLOCK
    tq = blocks * Q_BLOCK
    prev_blk = lambda i: jnp.maximum(i * blocks - 1, 0)
    next_blk = lambda i: jnp.minimum((i + 1) * blocks, nb - 1)
    return pl.pallas_call(
        functools.partial(_attn_b_kernel, seq=seq, layer=layer, blocks=blocks),
        grid=(batch, seq // tq),
        in_specs=[
            pl.BlockSpec(memory_space=pltpu.SMEM),
            pl.BlockSpec((1, Q_W, tq), lambda b, i: (b, 0, i)),
            pl.BlockSpec((1, Q_BLOCK, KV_W), lambda b, i: (b, prev_blk(i), 0)),
            pl.BlockSpec((1, tq, KV_W), lambda b, i: (b, i, 0)),
            pl.BlockSpec((1, Q_BLOCK, KV_W), lambda b, i: (b, next_blk(i), 0)),
            pl.BlockSpec((1, KV_HEADS, VX_ROWS, Q_BLOCK), lambda b, i: (b, 0, 0, prev_blk(i))),
            pl.BlockSpec((1, KV_HEADS, VX_ROWS, tq), lambda b, i: (b, 0, 0, i)),
            pl.BlockSpec((1, KV_HEADS, VX_ROWS, Q_BLOCK), lambda b, i: (b, 0, 0, next_blk(i))),
            pl.BlockSpec((KV_HEADS, BAND, GROUP * Q_BLOCK), lambda b, i: (0, 0, 0)),
        ],
        out_specs=pl.BlockSpec((1, tq, Q_W), lambda b, i: (b, i, 0)),
        out_shape=jax.ShapeDtypeStruct((batch, seq, Q_W), BF16),
        scratch_shapes=[pltpu.VMEM((2, BAND, GROUP * Q_BLOCK), F32)],
        compiler_params=_params("parallel", "parallel"),
        name="attn_window",
    )(sinks, qt_b, k_b, k_b, k_b, vx_b, vx_b, vx_b, band_bias)


def _rope_tables(seq):
    pos = jnp.arange(seq)
    row = (pos // GRID_W).astype(F32)
    col = (pos % GRID_W).astype(F32)
    axis_dim = HEAD_DIM // 2
    freqs = ROPE_THETA ** (-jnp.arange(0, axis_dim, 2, dtype=F32) / axis_dim)
    ang_r = row[:, None] * freqs[None, :]
    ang_c = col[:, None] * freqs[None, :]
    cr, sr, cc, sc = jnp.cos(ang_r), jnp.sin(ang_r), jnp.cos(ang_c), jnp.sin(ang_c)
    cos = jnp.concatenate([cr, cr, cc, cc], axis=1)
    sin = jnp.concatenate([-sr, sr, -sc, sc], axis=1)
    reps = LANES // HEAD_DIM
    return jnp.tile(cos, (1, reps)), jnp.tile(sin, (1, reps))


def _qkv_gains(g_qk):
    ones = jnp.ones((DEPTH, KV_W), F32)
    t = lambda i, n: jnp.tile(g_qk[:, i, :].astype(F32), (1, n))
    q_scale = SM_SCALE * LOG2E
    return jnp.concatenate([t(0, Q_HEADS) * q_scale, t(1, KV_HEADS), ones,
                            t(2, Q_HEADS) * q_scale, t(3, KV_HEADS), ones], axis=1)[:, None, :]


def _tiles(seq):
    tm_ffn = min(512, seq)
    tm_qkv = min(512, seq)
    tq = min(2 * Q_BLOCK, seq)
    tk = min(MXU_DIM, seq)
    tiles_per_trip = min(32, seq // tk)
    win_blocks = min(8, seq // Q_BLOCK)
    return tm_ffn, tm_qkv, tq, tk, tiles_per_trip, win_blocks


def _trunk(x, mod_all, weights, band_bias):
    seq = x.shape[1]
    tm_ffn, tm_qkv, tq, tk, tiles_per_trip, win_blocks = _tiles(seq)
    cos, sin = _rope_tables(seq)
    for l in range(DEPTH):
        mod = mod_all[l]
        x = _ffn(x, mod, weights["g_norm"], weights["ffn1_w_in"], weights["ffn1_w_out"], l, 0, tm_ffn)
        qt_a, k_a, vx_a, qt_b, k_b, vx_b = _qkv(x, mod, weights["g_norm"], weights["w_qkv"], weights["gains"],
                                                cos, sin, weights["ones_bd"], l, tm_qkv)
        out_a = _attn_a(qt_a, k_a, vx_a, tq, tk, tiles_per_trip)
        out_b = _attn_b(qt_b, k_b, vx_b, band_bias, weights["sinks"], l, win_blocks)
        x = _ffn(x, mod, weights["g_norm"], weights["ffn2_w_in"], weights["ffn2_w_out"], l, 2, tm_ffn,
                 mix=(out_a, out_b, weights["w_o"]))
    return x


def _prepare(w_ada, b_ada, g_norm, ffn1_w_in, ffn1_w_out, w_qkv, g_qk, attn_sinks, w_o,
             ffn2_w_in, ffn2_w_out):
    head_id = np.arange(MXU_DIM) // HEAD_DIM
    return {
        "g_norm": g_norm.astype(F32).reshape(DEPTH * 3, 1, D_MODEL),
        "ffn1_w_in": ffn1_w_in.astype(BF16), "ffn1_w_out": ffn1_w_out.astype(BF16),
        "ffn2_w_in": ffn2_w_in.astype(BF16), "ffn2_w_out": ffn2_w_out.astype(BF16),
        "w_qkv": w_qkv.astype(BF16), "w_o": w_o.astype(BF16),
        "gains": _qkv_gains(g_qk),
        "sinks": attn_sinks.astype(F32),
        "ones_bd": jnp.asarray(head_id[:, None] == head_id[None, :], BF16),
    }


def kernel(x_prompt, x_sample, c_prompt, c_sample, w_ada, b_ada, g_norm, ffn1_w_in, ffn1_w_out,
           w_qkv, g_qk, attn_sinks, rel_bias, w_o, ffn2_w_in, ffn2_w_out):
    weights = _prepare(w_ada, b_ada, g_norm, ffn1_w_in, ffn1_w_out, w_qkv, g_qk, attn_sinks, w_o,
                       ffn2_w_in, ffn2_w_out)
    nb_p, nb_s = c_prompt.shape[0], c_sample.shape[0]
    c_all = jnp.concatenate(
        [c_prompt, c_sample, jnp.zeros((C_ROWS - nb_p - nb_s, D_MODEL), F32)], axis=0)
    mod = _modulation(c_all, w_ada, b_ada).reshape(DEPTH, C_ROWS, N_MOD, D_MODEL)
    band_bias = _band_bias(rel_bias)
    y_prompt = _trunk(x_prompt, mod[:, :nb_p], weights, band_bias)
    y_sample = _trunk(x_sample, mod[:, nb_p:nb_p + nb_s], weights, band_bias)
    return (y_prompt, y_sample)
```

```python
import functools
import math

import jax
import jax.numpy as jnp
import numpy as np
from jax import lax
from jax.experimental import pallas as pl
from jax.experimental.pallas import tpu as pltpu

F32 = jnp.float32
BF16 = jnp.bfloat16

D_MODEL = 1024
DEPTH = 4
HEAD_DIM = 64
Q_HEADS = 8
KV_HEADS = 2
GROUP = Q_HEADS // KV_HEADS
Q_W = Q_HEADS * HEAD_DIM
KV_W = KV_HEADS * HEAD_DIM
QKV_W = 2 * (Q_W + 2 * KV_W)
D_FF = 2816
GRID_W = 64
Q_BLOCK = 128
WINDOW = 128
BAND = Q_BLOCK + 2 * WINDOW
NUM_BUCKETS = 32
MAX_DISTANCE = 128
ROPE_THETA = 10000.0
EPS = 1e-6
N_MOD = 9
SM_SCALE = HEAD_DIM ** -0.5
LOG2E = math.log2(math.e)
BF16_SUBLANES = 16
VX_ROWS = HEAD_DIM + BF16_SUBLANES

LANES = 128
MXU_DIM = 256
VMEM_LIMIT = 56 * 1024 * 1024

C_ROWS = 16


def _dot(a, b):
    return jnp.dot(a, b, preferred_element_type=F32)


def _silu(a):
    return a / (1.0 + jnp.exp(-a))


def _params(*sem):
    return pltpu.CompilerParams(dimension_semantics=sem, vmem_limit_bytes=VMEM_LIMIT)


def _mod_kernel(c_ref, w_ref, b_ref, o_ref):
    c = c_ref[...]
    o_ref[0] = _dot(_silu(c).astype(BF16), w_ref[0].astype(BF16)) + b_ref[0]


def _modulation(c_all, w_ada, b_ada):
    tn = D_MODEL
    n_out = N_MOD * D_MODEL
    return pl.pallas_call(
        _mod_kernel,
        grid=(DEPTH, n_out // tn),
        in_specs=[
            pl.BlockSpec((C_ROWS, D_MODEL), lambda l, j: (0, 0)),
            pl.BlockSpec((1, D_MODEL, tn), lambda l, j: (l, 0, j)),
            pl.BlockSpec((1, 1, tn), lambda l, j: (l, 0, j)),
        ],
        out_specs=pl.BlockSpec((1, C_ROWS, tn), lambda l, j: (l, 0, j)),
        out_shape=jax.ShapeDtypeStruct((DEPTH, C_ROWS, n_out), F32),
        compiler_params=_params("parallel", "parallel"),
        name="adaln_mod",
    )(c_all, w_ada, b_ada.reshape(DEPTH, 1, n_out))


def _bias_kernel(rb_ref, bucket_ref, o_ref):
    bucket = bucket_ref[...]
    for h in range(Q_HEADS):
        acc = jnp.full(bucket.shape, -jnp.inf, F32)
        for b in range(NUM_BUCKETS):
            acc = jnp.where(bucket == b, rb_ref[b, h] * LOG2E, acc)
        g = h % GROUP
        o_ref[h // GROUP, :, g * Q_BLOCK:(g + 1) * Q_BLOCK] = acc


def _t5_bucket(rel):
    half = NUM_BUCKETS // 2
    max_exact = half // 2
    base = jnp.where(rel > 0, half, 0)
    n = jnp.abs(rel)
    nf = jnp.maximum(n, 1).astype(F32)
    large = max_exact + (jnp.log(nf / max_exact) / math.log(MAX_DISTANCE / max_exact)
                         * (half - max_exact)).astype(jnp.int32)
    large = jnp.minimum(large, half - 1)
    return (base + jnp.where(n < max_exact, n, large)).astype(jnp.int32)


def _band_bias(rel_bias):
    kk = jnp.arange(BAND)[:, None]
    qq = jnp.arange(Q_BLOCK)[None, :]
    rel = kk - WINDOW - qq
    bucket = jnp.where(jnp.abs(rel) <= WINDOW, _t5_bucket(rel), -1)
    return pl.pallas_call(
        _bias_kernel,
        in_specs=[
            pl.BlockSpec(memory_space=pltpu.SMEM),
            pl.BlockSpec((BAND, Q_BLOCK), lambda: (0, 0)),
        ],
        out_specs=pl.BlockSpec((KV_HEADS, BAND, GROUP * Q_BLOCK), lambda: (0, 0, 0)),
        out_shape=jax.ShapeDtypeStruct((KV_HEADS, BAND, GROUP * Q_BLOCK), F32),
        name="band_bias",
    )(rel_bias.astype(F32), bucket)


def _ada_norm(x, g, scale, shift):
    y = x * lax.rsqrt(jnp.mean(x * x, axis=-1, keepdims=True) + EPS)
    return y * (g * (1.0 + scale)) + shift


def _ffn_kernel(x_ref, mod_ref, g_ref, wi_ref, wo_ref, *rest, rows, mix_gate_row):
    sh_i, sc_i, gt_i = rows
    x = x_ref[0]
    if mix_gate_row is None:
        (o_ref,) = rest
    else:
        oa_ref, ob_ref, wmix_ref, o_ref = rest
        mix = jnp.concatenate([oa_ref[0], ob_ref[0]], axis=1)
        x = x + mod_ref[0, mix_gate_row:mix_gate_row + 1, :] * _dot(mix, wmix_ref[0])
    h = _ada_norm(x, g_ref[0], mod_ref[0, sc_i:sc_i + 1, :], mod_ref[0, sh_i:sh_i + 1, :]).astype(BF16)
    acc = jnp.zeros(x.shape, F32)
    for c in range(D_FF // MXU_DIM):
        lo = c * MXU_DIM
        a = _dot(h, wi_ref[0, :, lo:lo + MXU_DIM])
        b = _dot(h, wi_ref[0, :, D_FF + lo:D_FF + lo + MXU_DIM])
        acc = acc + _dot((_silu(a) * b).astype(BF16), wo_ref[0, lo:lo + MXU_DIM, :])
    o_ref[0] = x + 0.5 * mod_ref[0, gt_i:gt_i + 1, :] * acc


def _ffn(x, mod, g_norm, w_in, w_out, layer, sub, tm, mix=None):
    batch, seq, _ = x.shape
    rows = (3 * sub, 3 * sub + 1, 3 * sub + 2)
    resident = pl.Buffered(1)
    in_specs = [
        pl.BlockSpec((1, tm, D_MODEL), lambda b, i: (b, i, 0)),
        pl.BlockSpec((1, N_MOD, D_MODEL), lambda b, i: (b, 0, 0)),
        pl.BlockSpec((1, 1, D_MODEL), lambda b, i: (3 * layer + sub, 0, 0)),
        pl.BlockSpec((1, D_MODEL, 2 * D_FF), lambda b, i: (layer, 0, 0), pipeline_mode=resident),
        pl.BlockSpec((1, D_FF, D_MODEL), lambda b, i: (layer, 0, 0), pipeline_mode=resident),
    ]
    operands = [x, mod, g_norm, w_in, w_out]
    if mix is not None:
        in_specs += [
            pl.BlockSpec((1, tm, Q_W), lambda b, i: (b, i, 0)),
            pl.BlockSpec((1, tm, Q_W), lambda b, i: (b, i, 0)),
            pl.BlockSpec((1, 2 * Q_W, D_MODEL), lambda b, i: (layer, 0, 0), pipeline_mode=resident),
        ]
        operands += list(mix)
    return pl.pallas_call(
        functools.partial(_ffn_kernel, rows=rows, mix_gate_row=None if mix is None else 3 * sub - 1),
        grid=(batch, seq // tm),
        in_specs=in_specs,
        out_specs=pl.BlockSpec((1, tm, D_MODEL), lambda b, i: (b, i, 0)),
        out_shape=jax.ShapeDtypeStruct(x.shape, F32),
        compiler_params=_params("parallel", "parallel"),
        name="ffn",
    )(*operands)


def _head_sumsq(t, ones_bd):
    return _dot((t * t).astype(BF16), ones_bd)


def _rope(t, cos, sin, first_half):
    up = pltpu.roll(t, LANES - HEAD_DIM // 4, axis=1)
    dn = pltpu.roll(t, HEAD_DIM // 4, axis=1)
    return t * cos + jnp.where(first_half, up, dn) * sin


def _store_vx(vx_ref, v_t):
    ones = jnp.ones((VX_ROWS - HEAD_DIM, v_t.shape[1]), F32)
    for kv in range(KV_HEADS):
        vx_ref[0, kv] = jnp.concatenate(
            [v_t[kv * HEAD_DIM:(kv + 1) * HEAD_DIM], ones], axis=0).astype(BF16)


def _qkv_kernel(x_ref, mod_ref, g_ref, w_ref, gain_ref, cos_ref, sin_ref, ones_ref,
                qta_ref, ka_ref, vxa_ref, qtb_ref, kb_ref, vxb_ref):
    tm = x_ref.shape[1]
    h = _ada_norm(x_ref[0], g_ref[0], mod_ref[0, 4:5, :], mod_ref[0, 3:4, :]).astype(BF16)
    qkv = _dot(h, w_ref[0])
    ones_bd = ones_ref[...]
    gain = gain_ref[0]

    normed = []
    for s in range(QKV_W // MXU_DIM):
        t = qkv[:, s * MXU_DIM:(s + 1) * MXU_DIM]
        ss = _head_sumsq(t, ones_bd)
        normed.append(t * lax.rsqrt(ss * (1.0 / HEAD_DIM) + EPS)
                      * gain[:, s * MXU_DIM:(s + 1) * MXU_DIM])
    normed = jnp.concatenate(normed, axis=1)

    cos = cos_ref[...]
    sin = sin_ref[...]
    lane = lax.broadcasted_iota(jnp.int32, (tm, LANES), 1)
    first_half = (lane % (HEAD_DIM // 2)) < (HEAD_DIM // 4)

    qa = [_rope(normed[:, s * LANES:(s + 1) * LANES], cos, sin, first_half) * (SM_SCALE * LOG2E)
          for s in range(Q_W // LANES)]
    qa = jnp.concatenate(qa, axis=1)
    qta_ref[0] = qa.T.astype(BF16)
    ka_ref[0] = _rope(normed[:, Q_W:Q_W + KV_W], cos, sin, first_half).astype(BF16)
    _store_vx(vxa_ref, qkv[:, Q_W + KV_W:Q_W + 2 * KV_W].T)

    off = Q_W + 2 * KV_W
    qtb_ref[0] = (normed[:, off:off + Q_W] * (SM_SCALE * LOG2E)).T.astype(BF16)
    kb_ref[0] = normed[:, off + Q_W:off + Q_W + KV_W].astype(BF16)
    _store_vx(vxb_ref, qkv[:, off + Q_W + KV_W:off + Q_W + 2 * KV_W].T)


def _qkv(x, mod, g_norm, w_qkv, gains, cos, sin, ones_bd, layer, tm):
    batch, seq, _ = x.shape
    return pl.pallas_call(
        _qkv_kernel,
        grid=(batch, seq // tm),
        in_specs=[
            pl.BlockSpec((1, tm, D_MODEL), lambda b, i: (b, i, 0)),
            pl.BlockSpec((1, N_MOD, D_MODEL), lambda b, i: (b, 0, 0)),
            pl.BlockSpec((1, 1, D_MODEL), lambda b, i: (3 * layer + 1, 0, 0)),
            pl.BlockSpec((1, D_MODEL, QKV_W), lambda b, i: (layer, 0, 0)),
            pl.BlockSpec((1, 1, QKV_W), lambda b, i: (layer, 0, 0)),
            pl.BlockSpec((tm, LANES), lambda b, i: (i, 0)),
            pl.BlockSpec((tm, LANES), lambda b, i: (i, 0)),
            pl.BlockSpec((MXU_DIM, MXU_DIM), lambda b, i: (0, 0)),
        ],
        out_specs=2 * [
            pl.BlockSpec((1, Q_W, tm), lambda b, i: (b, 0, i)),
            pl.BlockSpec((1, tm, KV_W), lambda b, i: (b, i, 0)),
            pl.BlockSpec((1, KV_HEADS, VX_ROWS, tm), lambda b, i: (b, 0, 0, i)),
        ],
        out_shape=2 * [
            jax.ShapeDtypeStruct((batch, Q_W, seq), BF16),
            jax.ShapeDtypeStruct((batch, seq, KV_W), BF16),
            jax.ShapeDtypeStruct((batch, KV_HEADS, VX_ROWS, seq), BF16),
        ],
        compiler_params=_params("parallel", "parallel"),
        name="qkv",
    )(x, mod, g_norm, w_qkv, gains, cos, sin, ones_bd)


def _attn_a_kernel(qt_ref, k_ref, vx_ref, o_ref, s_scr, *, seq, tk, tiles_per_trip):
    tq = qt_ref.shape[2]
    nq = GROUP * tq
    gw = GROUP * HEAD_DIM
    n_tiles = seq // tk

    def sweep(kv, _):
        col = pl.multiple_of(kv * gw, gw)
        q4 = qt_ref[0, pl.ds(col, gw), :]
        qt = jnp.concatenate([q4[g * HEAD_DIM:(g + 1) * HEAD_DIM] for g in range(GROUP)], axis=1)
        zero = jnp.zeros_like(qt)
        q_pad = jnp.concatenate([jnp.where(kv == h, qt, zero) for h in range(KV_HEADS)], axis=0)

        def scores(j, slot, m):
            off = pl.multiple_of(jnp.minimum(j, n_tiles - 1) * tk, tk)
            s_t = _dot(k_ref[0, pl.ds(off, tk), :], q_pad)
            s_scr[slot] = s_t
            return jnp.maximum(m, jnp.max(s_t, axis=0, keepdims=True))

        def accumulate(j, slot, m_before, m_now, acc):
            off = pl.multiple_of(j * tk, tk)
            alpha = jnp.exp2(m_before - m_now)
            p_t = jnp.exp2(s_scr[slot] - m_now).astype(BF16)
            return alpha * acc + _dot(vx_ref[0, kv, :, pl.ds(off, tk)], p_t)

        def body(trip, carry):
            m_prev, m_cur, acc = carry
            j = tiles_per_trip * trip
            for u in range(tiles_per_trip):
                m_next = scores(j + u + 1, (u + 1) % 2, m_cur)
                acc = accumulate(j + u, u % 2, m_prev, m_cur, acc)
                m_prev, m_cur = m_cur, m_next
            return m_prev, m_cur, acc

        m0 = jnp.full((1, nq), -jnp.inf, F32)
        m1 = scores(0, 0, m0)
        acc0 = jnp.zeros((VX_ROWS, nq), F32)
        _, _, acc = lax.fori_loop(0, n_tiles // tiles_per_trip, body, (m0, m1, acc0))
        out_t = acc[:HEAD_DIM] / acc[HEAD_DIM:HEAD_DIM + 1]
        out_t = jnp.concatenate([out_t[:, g * tq:(g + 1) * tq] for g in range(GROUP)], axis=0)
        o_ref[0, :, pl.ds(col, gw)] = out_t.T.astype(BF16)
        return 0

    lax.fori_loop(0, KV_HEADS, sweep, 0)


def _attn_a(qt_a, k_a, vx_a, tq, tk, tiles_per_trip):
    batch, _, seq = qt_a.shape
    assert tiles_per_trip % 2 == 0 and (seq // tk) % tiles_per_trip == 0
    return pl.pallas_call(
        functools.partial(_attn_a_kernel, seq=seq, tk=tk, tiles_per_trip=tiles_per_trip),
        grid=(batch, seq // tq),
        in_specs=[
            pl.BlockSpec((1, Q_W, tq), lambda b, i: (b, 0, i)),
            pl.BlockSpec((1, seq, KV_W), lambda b, i: (b, 0, 0)),
            pl.BlockSpec((1, KV_HEADS, VX_ROWS, seq), lambda b, i: (b, 0, 0, 0)),
        ],
        out_specs=pl.BlockSpec((1, tq, Q_W), lambda b, i: (b, i, 0)),
        out_shape=jax.ShapeDtypeStruct((batch, seq, Q_W), BF16),
        scratch_shapes=[pltpu.VMEM((2, tk, GROUP * tq), F32)],
        compiler_params=_params("parallel", "parallel"),
        name="attn_global",
    )(qt_a, k_a, vx_a)


def _attn_b_kernel(sink_ref, qt_ref, kp_ref, kc_ref, kn_ref, vp_ref, vc_ref, vn_ref, bias_ref, o_ref, s_scr,
                   *, seq, layer, blocks):
    first_block = pl.program_id(1) * blocks
    n_blocks = seq // Q_BLOCK
    gw = GROUP * HEAD_DIM
    nq = GROUP * Q_BLOCK
    k_all = jnp.concatenate([kp_ref[0], kc_ref[0], kn_ref[0]], axis=0)
    lane_head = lax.broadcasted_iota(jnp.int32, (1, nq), 1) // Q_BLOCK
    zero = jnp.zeros((HEAD_DIM, nq), BF16)
    sinks = []
    for kv in range(KV_HEADS):
        sink = jnp.zeros((1, nq), F32)
        for g in range(GROUP):
            sink = jnp.where(lane_head == g, sink_ref[layer, kv * GROUP + g] * LOG2E, sink)
        sinks.append(sink)

    def scores(kv, blk, slot):
        lo = blk * Q_BLOCK
        q4 = qt_ref[0, kv * gw:(kv + 1) * gw, lo:lo + Q_BLOCK]
        qt = jnp.concatenate([q4[g * HEAD_DIM:(g + 1) * HEAD_DIM] for g in range(GROUP)], axis=1)
        q_pad = jnp.concatenate([qt, zero] if kv == 0 else [zero, qt], axis=0)
        s_t = _dot(k_all[lo:lo + BAND], q_pad) + bias_ref[kv]
        parts = [s_t[:Q_BLOCK], s_t[Q_BLOCK:2 * Q_BLOCK], s_t[2 * Q_BLOCK:]]
        if blk == 0:
            parts[0] = jnp.where(first_block > 0, parts[0], -jnp.inf)
        if blk == blocks - 1:
            parts[2] = jnp.where(first_block + blocks < n_blocks, parts[2], -jnp.inf)
        s_t = jnp.concatenate(parts, axis=0)
        s_scr[slot] = s_t
        return jnp.maximum(jnp.max(s_t, axis=0, keepdims=True), sinks[kv])

    def finish(kv, blk, slot, m):
        lo = blk * Q_BLOCK
        v_band = jnp.concatenate([vp_ref[0, kv], vc_ref[0, kv], vn_ref[0, kv]], axis=1)[:, lo:lo + BAND]
        p_t = jnp.exp2(s_scr[slot] - m).astype(BF16)
        acc = _dot(v_band, p_t)
        denom = acc[HEAD_DIM:HEAD_DIM + 1] + jnp.exp2(sinks[kv] - m)
        out_t = acc[:HEAD_DIM] / denom
        out_t = jnp.concatenate([out_t[:, g * Q_BLOCK:(g + 1) * Q_BLOCK] for g in range(GROUP)], axis=0)
        o_ref[0, lo:lo + Q_BLOCK, kv * gw:(kv + 1) * gw] = out_t.T.astype(BF16)

    chains = [(kv, blk) for kv in range(KV_HEADS) for blk in range(blocks)]
    m_next = scores(*chains[0], 0)
    for c, chain in enumerate(chains):
        m = m_next
        if c + 1 < len(chains):
            m_next = scores(*chains[c + 1], (c + 1) % 2)
        finish(*chain, c % 2, m)


def _attn_b(qt_b, k_b, vx_b, band_bias, sinks, layer, blocks):
    batch, _, seq = qt_b.shape
    nb = seq // Q_BLOCK
    tq = blocks * Q_BLOCK
    prev_blk = lambda i: jnp.maximum(i * blocks - 1, 0)
    next_blk = lambda i: jnp.minimum((i + 1) * blocks, nb - 1)
    return pl.pallas_call(
        functools.partial(_attn_b_kernel, seq=seq, layer=layer, blocks=blocks),
        grid=(batch, seq // tq),
        in_specs=[
            pl.BlockSpec(memory_space=pltpu.SMEM),
            pl.BlockSpec((1, Q_W, tq), lambda b, i: (b, 0, i)),
            pl.BlockSpec((1, Q_BLOCK, KV_W), lambda b, i: (b, prev_blk(i), 0)),
            pl.BlockSpec((1, tq, KV_W), lambda b, i: (b, i, 0)),
            pl.BlockSpec((1, Q_BLOCK, KV_W), lambda b, i: (b, next_blk(i), 0)),
            pl.BlockSpec((1, KV_HEADS, VX_ROWS, Q_BLOCK), lambda b, i: (b, 0, 0, prev_blk(i))),
            pl.BlockSpec((1, KV_HEADS, VX_ROWS, tq), lambda b, i: (b, 0, 0, i)),
            pl.BlockSpec((1, KV_HEADS, VX_ROWS, Q_BLOCK), lambda b, i: (b, 0, 0, next_blk(i))),
            pl.BlockSpec((KV_HEADS, BAND, GROUP * Q_BLOCK), lambda b, i: (0, 0, 0)),
        ],
        out_specs=pl.BlockSpec((1, tq, Q_W), lambda b, i: (b, i, 0)),
        out_shape=jax.ShapeDtypeStruct((batch, seq, Q_W), BF16),
        scratch_shapes=[pltpu.VMEM((2, BAND, GROUP * Q_BLOCK), F32)],
        compiler_params=_params("parallel", "parallel"),
        name="attn_window",
    )(sinks, qt_b, k_b, k_b, k_b, vx_b, vx_b, vx_b, band_bias)


def _rope_tables(seq):
    pos = jnp.arange(seq)
    row = (pos // GRID_W).astype(F32)
    col = (pos % GRID_W).astype(F32)
    axis_dim = HEAD_DIM // 2
    freqs = ROPE_THETA ** (-jnp.arange(0, axis_dim, 2, dtype=F32) / axis_dim)
    ang_r = row[:, None] * freqs[None, :]
    ang_c = col[:, None] * freqs[None, :]
    cr, sr, cc, sc = jnp.cos(ang_r), jnp.sin(ang_r), jnp.cos(ang_c), jnp.sin(ang_c)
    cos = jnp.concatenate([cr, cr, cc, cc], axis=1)
    sin = jnp.concatenate([-sr, sr, -sc, sc], axis=1)
    reps = LANES // HEAD_DIM
    return jnp.tile(cos, (1, reps)), jnp.tile(sin, (1, reps))


def _qkv_gains(g_qk):
    ones = jnp.ones((DEPTH, KV_W), F32)
    t = lambda i, n: jnp.tile(g_qk[:, i, :].astype(F32), (1, n))
    return jnp.concatenate([t(0, Q_HEADS), t(1, KV_HEADS), ones,
                            t(2, Q_HEADS), t(3, KV_HEADS), ones], axis=1)[:, None, :]


def _tiles(seq):
    tm_ffn = min(512, seq)
    tm_qkv = min(512, seq)
    tq = min(2 * Q_BLOCK, seq)
    tk = min(MXU_DIM, seq)
    tiles_per_trip = min(32, seq // tk)
    win_blocks = min(8, seq // Q_BLOCK)
    return tm_ffn, tm_qkv, tq, tk, tiles_per_trip, win_blocks


def _trunk(x, mod_all, weights, band_bias):
    seq = x.shape[1]
    tm_ffn, tm_qkv, tq, tk, tiles_per_trip, win_blocks = _tiles(seq)
    cos, sin = _rope_tables(seq)
    for l in range(DEPTH):
        mod = mod_all[l]
        x = _ffn(x, mod, weights["g_norm"], weights["ffn1_w_in"], weights["ffn1_w_out"], l, 0, tm_ffn)
        qt_a, k_a, vx_a, qt_b, k_b, vx_b = _qkv(x, mod, weights["g_norm"], weights["w_qkv"], weights["gains"],
                                                cos, sin, weights["ones_bd"], l, tm_qkv)
        out_a = _attn_a(qt_a, k_a, vx_a, tq, tk, tiles_per_trip)
        out_b = _attn_b(qt_b, k_b, vx_b, band_bias, weights["sinks"], l, win_blocks)
        x = _ffn(x, mod, weights["g_norm"], weights["ffn2_w_in"], weights["ffn2_w_out"], l, 2, tm_ffn,
                 mix=(out_a, out_b, weights["w_o"]))
    return x


def _prepare(w_ada, b_ada, g_norm, ffn1_w_in, ffn1_w_out, w_qkv, g_qk, attn_sinks, w_o,
             ffn2_w_in, ffn2_w_out):
    head_id = np.arange(MXU_DIM) // HEAD_DIM
    return {
        "g_norm": g_norm.astype(F32).reshape(DEPTH * 3, 1, D_MODEL),
        "ffn1_w_in": ffn1_w_in.astype(BF16), "ffn1_w_out": ffn1_w_out.astype(BF16),
        "ffn2_w_in": ffn2_w_in.astype(BF16), "ffn2_w_out": ffn2_w_out.astype(BF16),
        "w_qkv": w_qkv.astype(BF16), "w_o": w_o.astype(BF16),
        "gains": _qkv_gains(g_qk),
        "sinks": attn_sinks.astype(F32),
        "ones_bd": jnp.asarray(head_id[:, None] == head_id[None, :], BF16),
    }


def kernel(x_prompt, x_sample, c_prompt, c_sample, w_ada, b_ada, g_norm, ffn1_w_in, ffn1_w_out,
           w_qkv, g_qk, attn_sinks, rel_bias, w_o, ffn2_w_in, ffn2_w_out):
    weights = _prepare(w_ada, b_ada, g_norm, ffn1_w_in, ffn1_w_out, w_qkv, g_qk, attn_sinks, w_o,
                       ffn2_w_in, ffn2_w_out)
    nb_p, nb_s = c_prompt.shape[0], c_sample.shape[0]
    c_all = jnp.concatenate(
        [c_prompt, c_sample, jnp.zeros((C_ROWS - nb_p - nb_s, D_MODEL), F32)], axis=0)
    mod = _modulation(c_all, w_ada, b_ada).reshape(DEPTH, C_ROWS, N_MOD, D_MODEL)
    band_bias = _band_bias(rel_bias)
    y_prompt = _trunk(x_prompt, mod[:, :nb_p], weights, band_bias)
    y_sample = _trunk(x_sample, mod[:, nb_p:nb_p + nb_s], weights, band_bias)
    return (y_prompt, y_sample)
```

```python
import functools
import math

import jax
import jax.numpy as jnp
import numpy as np
from jax import lax
from jax.experimental import pallas as pl
from jax.experimental.pallas import tpu as pltpu

F32 = jnp.float32
BF16 = jnp.bfloat16

D_MODEL = 1024
DEPTH = 4
HEAD_DIM = 64
Q_HEADS = 8
KV_HEADS = 2
GROUP = Q_HEADS // KV_HEADS
Q_W = Q_HEADS * HEAD_DIM
KV_W = KV_HEADS * HEAD_DIM
QKV_W = 2 * (Q_W + 2 * KV_W)
D_FF = 2816
GRID_W = 64
Q_BLOCK = 128
WINDOW = 128
BAND = Q_BLOCK + 2 * WINDOW
NUM_BUCKETS = 32
MAX_DISTANCE = 128
ROPE_THETA = 10000.0
EPS = 1e-6
N_MOD = 9
SM_SCALE = HEAD_DIM ** -0.5
LOG2E = math.log2(math.e)
BF16_SUBLANES = 16
VX_ROWS = HEAD_DIM + BF16_SUBLANES

LANES = 128
MXU_DIM = 256
VMEM_LIMIT = 56 * 1024 * 1024

C_ROWS = 16


def _dot(a, b):
    return jnp.dot(a, b, preferred_element_type=F32)


def _silu(a):
    return a / (1.0 + jnp.exp(-a))


def _params(*sem):
    return pltpu.CompilerParams(dimension_semantics=sem, vmem_limit_bytes=VMEM_LIMIT)


def _mod_kernel(c_ref, w_ref, b_ref, o_ref):
    c = c_ref[...]
    o_ref[0] = _dot(_silu(c).astype(BF16), w_ref[0].astype(BF16)) + b_ref[0]


def _modulation(c_all, w_ada, b_ada):
    tn = D_MODEL
    n_out = N_MOD * D_MODEL
    return pl.pallas_call(
        _mod_kernel,
        grid=(DEPTH, n_out // tn),
        in_specs=[
            pl.BlockSpec((C_ROWS, D_MODEL), lambda l, j: (0, 0)),
            pl.BlockSpec((1, D_MODEL, tn), lambda l, j: (l, 0, j)),
            pl.BlockSpec((1, 1, tn), lambda l, j: (l, 0, j)),
        ],
        out_specs=pl.BlockSpec((1, C_ROWS, tn), lambda l, j: (l, 0, j)),
        out_shape=jax.ShapeDtypeStruct((DEPTH, C_ROWS, n_out), F32),
        compiler_params=_params("parallel", "parallel"),
        name="adaln_mod",
    )(c_all, w_ada, b_ada.reshape(DEPTH, 1, n_out))


def _bias_kernel(rb_ref, bucket_ref, o_ref):
    bucket = bucket_ref[...]
    for h in range(Q_HEADS):
        acc = jnp.full(bucket.shape, -jnp.inf, F32)
        for b in range(NUM_BUCKETS):
            acc = jnp.where(bucket == b, rb_ref[b, h] * LOG2E, acc)
        g = h % GROUP
        o_ref[h // GROUP, :, g * Q_BLOCK:(g + 1) * Q_BLOCK] = acc


def _t5_bucket(rel):
    half = NUM_BUCKETS // 2
    max_exact = half // 2
    base = jnp.where(rel > 0, half, 0)
    n = jnp.abs(rel)
    nf = jnp.maximum(n, 1).astype(F32)
    large = max_exact + (jnp.log(nf / max_exact) / math.log(MAX_DISTANCE / max_exact)
                         * (half - max_exact)).astype(jnp.int32)
    large = jnp.minimum(large, half - 1)
    return (base + jnp.where(n < max_exact, n, large)).astype(jnp.int32)


def _band_bias(rel_bias):
    kk = jnp.arange(BAND)[:, None]
    qq = jnp.arange(Q_BLOCK)[None, :]
    rel = kk - WINDOW - qq
    bucket = jnp.where(jnp.abs(rel) <= WINDOW, _t5_bucket(rel), -1)
    return pl.pallas_call(
        _bias_kernel,
        in_specs=[
            pl.BlockSpec(memory_space=pltpu.SMEM),
            pl.BlockSpec((BAND, Q_BLOCK), lambda: (0, 0)),
        ],
        out_specs=pl.BlockSpec((KV_HEADS, BAND, GROUP * Q_BLOCK), lambda: (0, 0, 0)),
        out_shape=jax.ShapeDtypeStruct((KV_HEADS, BAND, GROUP * Q_BLOCK), F32),
        name="band_bias",
    )(rel_bias.astype(F32), bucket)


def _ada_norm(x, g, scale, shift):
    y = x * lax.rsqrt(jnp.mean(x * x, axis=-1, keepdims=True) + EPS)
    return y * (g * (1.0 + scale)) + shift


def _ffn_kernel(x_ref, mod_ref, g_ref, wi_ref, wo_ref, *rest, rows, mix_gate_row):
    sh_i, sc_i, gt_i = rows
    x = x_ref[0]
    if mix_gate_row is None:
        (o_ref,) = rest
    else:
        oa_ref, ob_ref, wmix_ref, o_ref = rest
        mix = jnp.concatenate([oa_ref[0], ob_ref[0]], axis=1)
        x = x + mod_ref[0, mix_gate_row:mix_gate_row + 1, :] * _dot(mix, wmix_ref[0])
    h = _ada_norm(x, g_ref[0], mod_ref[0, sc_i:sc_i + 1, :], mod_ref[0, sh_i:sh_i + 1, :]).astype(BF16)
    acc = jnp.zeros(x.shape, F32)
    for c in range(D_FF // MXU_DIM):
        lo = c * MXU_DIM
        a = _dot(h, wi_ref[0, :, lo:lo + MXU_DIM])
        b = _dot(h, wi_ref[0, :, D_FF + lo:D_FF + lo + MXU_DIM])
        acc = acc + _dot((_silu(a) * b).astype(BF16), wo_ref[0, lo:lo + MXU_DIM, :])
    o_ref[0] = x + 0.5 * mod_ref[0, gt_i:gt_i + 1, :] * acc


def _ffn(x, mod, g_norm, w_in, w_out, layer, sub, tm, mix=None):
    batch, seq, _ = x.shape
    rows = (3 * sub, 3 * sub + 1, 3 * sub + 2)
    resident = pl.Buffered(1)
    in_specs = [
        pl.BlockSpec((1, tm, D_MODEL), lambda b, i: (b, i, 0)),
        pl.BlockSpec((1, N_MOD, D_MODEL), lambda b, i: (b, 0, 0)),
        pl.BlockSpec((1, 1, D_MODEL), lambda b, i: (3 * layer + sub, 0, 0)),
        pl.BlockSpec((1, D_MODEL, 2 * D_FF), lambda b, i: (layer, 0, 0), pipeline_mode=resident),
        pl.BlockSpec((1, D_FF, D_MODEL), lambda b, i: (layer, 0, 0), pipeline_mode=resident),
    ]
    operands = [x, mod, g_norm, w_in, w_out]
    if mix is not None:
        in_specs += [
            pl.BlockSpec((1, tm, Q_W), lambda b, i: (b, i, 0)),
            pl.BlockSpec((1, tm, Q_W), lambda b, i: (b, i, 0)),
            pl.BlockSpec((1, 2 * Q_W, D_MODEL), lambda b, i: (layer, 0, 0), pipeline_mode=resident),
        ]
        operands += list(mix)
    return pl.pallas_call(
        functools.partial(_ffn_kernel, rows=rows, mix_gate_row=None if mix is None else 3 * sub - 1),
        grid=(batch, seq // tm),
        in_specs=in_specs,
        out_specs=pl.BlockSpec((1, tm, D_MODEL), lambda b, i: (b, i, 0)),
        out_shape=jax.ShapeDtypeStruct(x.shape, F32),
        compiler_params=_params("parallel", "parallel"),
        name="ffn",
    )(*operands)


def _head_sumsq(t, ones_bd):
    return _dot((t * t).astype(BF16), ones_bd)


def _rope(t, cos, sin, first_half):
    up = pltpu.roll(t, LANES - HEAD_DIM // 4, axis=1)
    dn = pltpu.roll(t, HEAD_DIM // 4, axis=1)
    return t * cos + jnp.where(first_half, up, dn) * sin


def _store_vx(vx_ref, v_t):
    ones = jnp.ones((VX_ROWS - HEAD_DIM, v_t.shape[1]), F32)
    for kv in range(KV_HEADS):
        vx_ref[0, kv] = jnp.concatenate(
            [v_t[kv * HEAD_DIM:(kv + 1) * HEAD_DIM], ones], axis=0).astype(BF16)


def _qkv_kernel(x_ref, mod_ref, g_ref, w_ref, gain_ref, cos_ref, sin_ref, ones_ref,
                qta_ref, ka_ref, vxa_ref, qtb_ref, kb_ref, vxb_ref):
    tm = x_ref.shape[1]
    h = _ada_norm(x_ref[0], g_ref[0], mod_ref[0, 4:5, :], mod_ref[0, 3:4, :]).astype(BF16)
    qkv = _dot(h, w_ref[0])
    ones_bd = ones_ref[...]
    gain = gain_ref[0]

    normed = []
    for s in range(QKV_W // MXU_DIM):
        t = qkv[:, s * MXU_DIM:(s + 1) * MXU_DIM]
        ss = _head_sumsq(t, ones_bd)
        normed.append(t * lax.rsqrt(ss * (1.0 / HEAD_DIM) + EPS)
                      * gain[:, s * MXU_DIM:(s + 1) * MXU_DIM])
    normed = jnp.concatenate(normed, axis=1)

    cos = cos_ref[...]
    sin = sin_ref[...]
    lane = lax.broadcasted_iota(jnp.int32, (tm, LANES), 1)
    first_half = (lane % (HEAD_DIM // 2)) < (HEAD_DIM // 4)

    qa = [_rope(normed[:, s * LANES:(s + 1) * LANES], cos, sin, first_half) * (SM_SCALE * LOG2E)
          for s in range(Q_W // LANES)]
    qa = jnp.concatenate(qa, axis=1)
    qta_ref[0] = qa.T.astype(BF16)
    ka_ref[0] = _rope(normed[:, Q_W:Q_W + KV_W], cos, sin, first_half).astype(BF16)
    _store_vx(vxa_ref, qkv[:, Q_W + KV_W:Q_W + 2 * KV_W].T)

    off = Q_W + 2 * KV_W
    qtb_ref[0] = (normed[:, off:off + Q_W] * (SM_SCALE * LOG2E)).T.astype(BF16)
    kb_ref[0] = normed[:, off + Q_W:off + Q_W + KV_W].astype(BF16)
    _store_vx(vxb_ref, qkv[:, off + Q_W + KV_W:off + Q_W + 2 * KV_W].T)


def _qkv(x, mod, g_norm, w_qkv, gains, cos, sin, ones_bd, layer, tm):
    batch, seq, _ = x.shape
    return pl.pallas_call(
        _qkv_kernel,
        grid=(batch, seq // tm),
        in_specs=[
            pl.BlockSpec((1, tm, D_MODEL), lambda b, i: (b, i, 0)),
            pl.BlockSpec((1, N_MOD, D_MODEL), lambda b, i: (b, 0, 0)),
            pl.BlockSpec((1, 1, D_MODEL), lambda b, i: (3 * layer + 1, 0, 0)),
            pl.BlockSpec((1, D_MODEL, QKV_W), lambda b, i: (layer, 0, 0)),
            pl.BlockSpec((1, 1, QKV_W), lambda b, i: (layer, 0, 0)),
            pl.BlockSpec((tm, LANES), lambda b, i: (i, 0)),
            pl.BlockSpec((tm, LANES), lambda b, i: (i, 0)),
            pl.BlockSpec((MXU_DIM, MXU_DIM), lambda b, i: (0, 0)),
        ],
        out_specs=2 * [
            pl.BlockSpec((1, Q_W, tm), lambda b, i: (b, 0, i)),
            pl.BlockSpec((1, tm, KV_W), lambda b, i: (b, i, 0)),
            pl.BlockSpec((1, KV_HEADS, VX_ROWS, tm), lambda b, i: (b, 0, 0, i)),
        ],
        out_shape=2 * [
            jax.ShapeDtypeStruct((batch, Q_W, seq), BF16),
            jax.ShapeDtypeStruct((batch, seq, KV_W), BF16),
            jax.ShapeDtypeStruct((batch, KV_HEADS, VX_ROWS, seq), BF16),
        ],
        compiler_params=_params("parallel", "parallel"),
        name="qkv",
    )(x, mod, g_norm, w_qkv, gains, cos, sin, ones_bd)


def _attn_a_kernel(qt_ref, k_ref, vx_ref, o_ref, s_scr, *, seq, tk, tiles_per_trip):
    tq = qt_ref.shape[2]
    nq = GROUP * tq
    gw = GROUP * HEAD_DIM
    n_tiles = seq // tk

    def sweep(kv, _):
        col = pl.multiple_of(kv * gw, gw)
        q4 = qt_ref[0, pl.ds(col, gw), :]
        qt = jnp.concatenate([q4[g * HEAD_DIM:(g + 1) * HEAD_DIM] for g in range(GROUP)], axis=1)
        zero = jnp.zeros_like(qt)
        q_pad = jnp.concatenate([jnp.where(kv == h, qt, zero) for h in range(KV_HEADS)], axis=0)

        def scores(j, slot, m):
            off = pl.multiple_of(jnp.minimum(j, n_tiles - 1) * tk, tk)
            s_t = _dot(k_ref[0, pl.ds(off, tk), :], q_pad)
            s_scr[slot] = s_t
            return jnp.maximum(m, jnp.max(s_t, axis=0, keepdims=True))

        def accumulate(j, slot, m_before, m_now, acc):
            off = pl.multiple_of(j * tk, tk)
            alpha = jnp.exp2(m_before - m_now)
            p_t = jnp.exp2(s_scr[slot] - m_now).astype(BF16)
            return alpha * acc + _dot(vx_ref[0, kv, :, pl.ds(off, tk)], p_t)

        def body(trip, carry):
            m_prev, m_cur, acc = carry
            j = tiles_per_trip * trip
            for u in range(tiles_per_trip):
                m_next = scores(j + u + 1, (u + 1) % 2, m_cur)
                acc = accumulate(j + u, u % 2, m_prev, m_cur, acc)
                m_prev, m_cur = m_cur, m_next
            return m_prev, m_cur, acc

        m0 = jnp.full((1, nq), -jnp.inf, F32)
        m1 = scores(0, 0, m0)
        acc0 = jnp.zeros((VX_ROWS, nq), F32)
        _, _, acc = lax.fori_loop(0, n_tiles // tiles_per_trip, body, (m0, m1, acc0))
        out_t = acc[:HEAD_DIM] / acc[HEAD_DIM:HEAD_DIM + 1]
        out_t = jnp.concatenate([out_t[:, g * tq:(g + 1) * tq] for g in range(GROUP)], axis=0)
        o_ref[0, :, pl.ds(col, gw)] = out_t.T.astype(BF16)
        return 0

    lax.fori_loop(0, KV_HEADS, sweep, 0)


def _attn_a(qt_a, k_a, vx_a, tq, tk, tiles_per_trip):
    batch, _, seq = qt_a.shape
    assert tiles_per_trip % 2 == 0 and (seq // tk) % tiles_per_trip == 0
    return pl.pallas_call(
        functools.partial(_attn_a_kernel, seq=seq, tk=tk, tiles_per_trip=tiles_per_trip),
        grid=(batch, seq // tq),
        in_specs=[
            pl.BlockSpec((1, Q_W, tq), lambda b, i: (b, 0, i)),
            pl.BlockSpec((1, seq, KV_W), lambda b, i: (b, 0, 0)),
            pl.BlockSpec((1, KV_HEADS, VX_ROWS, seq), lambda b, i: (b, 0, 0, 0)),
        ],
        out_specs=pl.BlockSpec((1, tq, Q_W), lambda b, i: (b, i, 0)),
        out_shape=jax.ShapeDtypeStruct((batch, seq, Q_W), BF16),
        scratch_shapes=[pltpu.VMEM((2, tk, GROUP * tq), F32)],
        compiler_params=_params("parallel", "parallel"),
        name="attn_global",
    )(qt_a, k_a, vx_a)


def _attn_b_kernel(sink_ref, qt_ref, kp_ref, kc_ref, kn_ref, vp_ref, vc_ref, vn_ref, bias_ref, o_ref, s_scr,
                   *, seq, layer, blocks):
    first_block = pl.program_id(1) * blocks
    n_blocks = seq // Q_BLOCK
    gw = GROUP * HEAD_DIM
    nq = GROUP * Q_BLOCK
    k_all = jnp.concatenate([kp_ref[0], kc_ref[0], kn_ref[0]], axis=0)
    lane_head = lax.broadcasted_iota(jnp.int32, (1, nq), 1) // Q_BLOCK
    zero = jnp.zeros((HEAD_DIM, nq), BF16)
    sinks = []
    for kv in range(KV_HEADS):
        sink = jnp.zeros((1, nq), F32)
        for g in range(GROUP):
            sink = jnp.where(lane_head == g, sink_ref[layer, kv * GROUP + g] * LOG2E, sink)
        sinks.append(sink)

    def scores(kv, blk, slot):
        lo = blk * Q_BLOCK
        q4 = qt_ref[0, kv * gw:(kv + 1) * gw, lo:lo + Q_BLOCK]
        qt = jnp.concatenate([q4[g * HEAD_DIM:(g + 1) * HEAD_DIM] for g in range(GROUP)], axis=1)
        q_pad = jnp.concatenate([qt, zero] if kv == 0 else [zero, qt], axis=0)
        s_t = _dot(k_all[lo:lo + BAND], q_pad) + bias_ref[kv]
        parts = [s_t[:Q_BLOCK], s_t[Q_BLOCK:2 * Q_BLOCK], s_t[2 * Q_BLOCK:]]
        if blk == 0:
            parts[0] = jnp.where(first_block > 0, parts[0], -jnp.inf)
        if blk == blocks - 1:
            parts[2] = jnp.where(first_block + blocks < n_blocks, parts[2], -jnp.inf)
        s_t = jnp.concatenate(parts, axis=0)
        s_scr[slot] = s_t
        return jnp.maximum(jnp.max(s_t, axis=0, keepdims=True), sinks[kv])

    def finish(kv, blk, slot, m):
        lo = blk * Q_BLOCK
        v_band = jnp.concatenate([vp_ref[0, kv], vc_ref[0, kv], vn_ref[0, kv]], axis=1)[:, lo:lo + BAND]
        p_t = jnp.exp2(s_scr[slot] - m).astype(BF16)
        acc = _dot(v_band, p_t)
        denom = acc[HEAD_DIM:HEAD_DIM + 1] + jnp.exp2(sinks[kv] - m)
        out_t = acc[:HEAD_DIM] / denom
        out_t = jnp.concatenate([out_t[:, g * Q_BLOCK:(g + 1) * Q_BLOCK] for g in range(GROUP)], axis=0)
        o_ref[0, lo:lo + Q_BLOCK, kv * gw:(kv + 1) * gw] = out_t.T.astype(BF16)

    chains = [(kv, blk) for kv in range(KV_HEADS) for blk in range(blocks)]
    m_next = scores(*chains[0], 0)
    for c, chain in enumerate(chains):
        m = m_next
        if c + 1 < len(chains):
            m_next = scores(*chains[c + 1], (c + 1) % 2)
        finish(*chain, c % 2, m)


def _attn_b(qt_b, k_b, vx_b, band_bias, sinks, layer, blocks):
    batch, _, seq = qt_b.shape
    nb = seq // Q_BLOCK
    tq = blocks * Q_BLOCK
    prev_blk = lambda i: jnp.maximum(i * blocks - 1, 0)
    next_blk = lambda i: jnp.minimum((i + 1) * blocks, nb - 1)
    return pl.pallas_call(
        functools.partial(_attn_b_kernel, seq=seq, layer=layer, blocks=blocks),
        grid=(batch, seq // tq),
        in_specs=[
            pl.BlockSpec(memory_space=pltpu.SMEM),
            pl.BlockSpec((1, Q_W, tq), lambda b, i: (b, 0, i)),
            pl.BlockSpec((1, Q_BLOCK, KV_W), lambda b, i: (b, prev_blk(i), 0)),
            pl.BlockSpec((1, tq, KV_W), lambda b, i: (b, i, 0)),
            pl.BlockSpec((1, Q_BLOCK, KV_W), lambda b, i: (b, next_blk(i), 0)),
            pl.BlockSpec((1, KV_HEADS, VX_ROWS, Q_BLOCK), lambda b, i: (b, 0, 0, prev_blk(i))),
            pl.BlockSpec((1, KV_HEADS, VX_ROWS, tq), lambda b, i: (b, 0, 0, i)),
            pl.BlockSpec((1, KV_HEADS, VX_ROWS, Q_BLOCK), lambda b, i: (b, 0, 0, next_blk(i))),
            pl.BlockSpec((KV_HEADS, BAND, GROUP * Q_BLOCK), lambda b, i: (0, 0, 0)),
        ],
        out_specs=pl.BlockSpec((1, tq, Q_W), lambda b, i: (b, i, 0)),
        out_shape=jax.ShapeDtypeStruct((batch, seq, Q_W), BF16),
        scratch_shapes=[pltpu.VMEM((2, BAND, GROUP * Q_BLOCK), F32)],
        compiler_params=_params("parallel", "parallel"),
        name="attn_window",
    )(sinks, qt_b, k_b, k_b, k_b, vx_b, vx_b, vx_b, band_bias)


def _rope_tables(seq):
    pos = jnp.arange(seq)
    row = (pos // GRID_W).astype(F32)
    col = (pos % GRID_W).astype(F32)
    axis_dim = HEAD_DIM // 2
    freqs = ROPE_THETA ** (-jnp.arange(0, axis_dim, 2, dtype=F32) / axis_dim)
    ang_r = row[:, None] * freqs[None, :]
    ang_c = col[:, None] * freqs[None, :]
    cr, sr, cc, sc = jnp.cos(ang_r), jnp.sin(ang_r), jnp.cos(ang_c), jnp.sin(ang_c)
    cos = jnp.concatenate([cr, cr, cc, cc], axis=1)
    sin = jnp.concatenate([-sr, sr, -sc, sc], axis=1)
    reps = LANES // HEAD_DIM
    return jnp.tile(cos, (1, reps)), jnp.tile(sin, (1, reps))


def _qkv_gains(g_qk):
    ones = jnp.ones((DEPTH, KV_W), F32)
    t = lambda i, n: jnp.tile(g_qk[:, i, :].astype(F32), (1, n))
    return jnp.concatenate([t(0, Q_HEADS), t(1, KV_HEADS), ones,
                            t(2, Q_HEADS), t(3, KV_HEADS), ones], axis=1)[:, None, :]


def _tiles(seq):
    tm_ffn = min(512, seq)
    tm_qkv = min(512, seq)
    tq = min(2 * Q_BLOCK, seq)
    tk = min(MXU_DIM, seq)
    tiles_per_trip = min(32, seq // tk)
    win_blocks = min(16, seq // Q_BLOCK)
    return tm_ffn, tm_qkv, tq, tk, tiles_per_trip, win_blocks


def _trunk(x, mod_all, weights, band_bias):
    seq = x.shape[1]
    tm_ffn, tm_qkv, tq, tk, tiles_per_trip, win_blocks = _tiles(seq)
    cos, sin = _rope_tables(seq)
    for l in range(DEPTH):
        mod = mod_all[l]
        x = _ffn(x, mod, weights["g_norm"], weights["ffn1_w_in"], weights["ffn1_w_out"], l, 0, tm_ffn)
        qt_a, k_a, vx_a, qt_b, k_b, vx_b = _qkv(x, mod, weights["g_norm"], weights["w_qkv"], weights["gains"],
                                                cos, sin, weights["ones_bd"], l, tm_qkv)
        out_a = _attn_a(qt_a, k_a, vx_a, tq, tk, tiles_per_trip)
        out_b = _attn_b(qt_b, k_b, vx_b, band_bias, weights["sinks"], l, win_blocks)
        x = _ffn(x, mod, weights["g_norm"], weights["ffn2_w_in"], weights["ffn2_w_out"], l, 2, tm_ffn,
                 mix=(out_a, out_b, weights["w_o"]))
    return x


def _prepare(g_norm, ffn1_w_in, ffn1_w_out, w_qkv, g_qk, attn_sinks, w_o, ffn2_w_in, ffn2_w_out):
    head_id = np.arange(MXU_DIM) // HEAD_DIM
    return {
        "g_norm": g_norm.astype(F32).reshape(DEPTH * 3, 1, D_MODEL),
        "ffn1_w_in": ffn1_w_in.astype(BF16), "ffn1_w_out": ffn1_w_out.astype(BF16),
        "ffn2_w_in": ffn2_w_in.astype(BF16), "ffn2_w_out": ffn2_w_out.astype(BF16),
        "w_qkv": w_qkv.astype(BF16), "w_o": w_o.astype(BF16),
        "gains": _qkv_gains(g_qk),
        "sinks": attn_sinks.astype(F32),
        "ones_bd": jnp.asarray(head_id[:, None] == head_id[None, :], BF16),
    }


def kernel(x_prompt, x_sample, c_prompt, c_sample, w_ada, b_ada, g_norm, ffn1_w_in, ffn1_w_out,
           w_qkv, g_qk, attn_sinks, rel_bias, w_o, ffn2_w_in, ffn2_w_out):
    weights = _prepare(g_norm, ffn1_w_in, ffn1_w_out, w_qkv, g_qk, attn_sinks, w_o, ffn2_w_in, ffn2_w_out)
    nb_p, nb_s = c_prompt.shape[0], c_sample.shape[0]
    c_all = jnp.concatenate(
        [c_prompt, c_sample, jnp.zeros((C_ROWS - nb_p - nb_s, D_MODEL), F32)], axis=0)
    mod = _modulation(c_all, w_ada, b_ada).reshape(DEPTH, C_ROWS, N_MOD, D_MODEL)
    band_bias = _band_bias(rel_bias)
    y_prompt = _trunk(x_prompt, mod[:, :nb_p], weights, band_bias)
    y_sample = _trunk(x_sample, mod[:, nb_p:nb_p + nb_s], weights, band_bias)
    return (y_prompt, y_sample)
```

```python
import functools
import math

import jax
import jax.numpy as jnp
import numpy as np
from jax import lax
from jax.experimental import pallas as pl
from jax.experimental.pallas import tpu as pltpu

F32 = jnp.float32
BF16 = jnp.bfloat16

D_MODEL = 1024
DEPTH = 4
HEAD_DIM = 64
Q_HEADS = 8
KV_HEADS = 2
GROUP = Q_HEADS // KV_HEADS
Q_W = Q_HEADS * HEAD_DIM
KV_W = KV_HEADS * HEAD_DIM
QKV_W = 2 * (Q_W + 2 * KV_W)
D_FF = 2816
GRID_W = 64
Q_BLOCK = 128
WINDOW = 128
BAND = Q_BLOCK + 2 * WINDOW
NUM_BUCKETS = 32
MAX_DISTANCE = 128
ROPE_THETA = 10000.0
EPS = 1e-6
N_MOD = 9
SM_SCALE = HEAD_DIM ** -0.5
LOG2E = math.log2(math.e)
BF16_SUBLANES = 16
VX_ROWS = HEAD_DIM + BF16_SUBLANES

LANES = 128
MXU_DIM = 256
VMEM_LIMIT = 56 * 1024 * 1024

C_ROWS = 16


def _dot(a, b):
    return jnp.dot(a, b, preferred_element_type=F32)


def _silu(a):
    return a / (1.0 + jnp.exp(-a))


def _params(*sem):
    return pltpu.CompilerParams(dimension_semantics=sem, vmem_limit_bytes=VMEM_LIMIT)


def _mod_kernel(c_ref, w_ref, b_ref, o_ref):
    c = c_ref[...]
    o_ref[0] = _dot(_silu(c).astype(BF16), w_ref[0].astype(BF16)) + b_ref[0]


def _modulation(c_all, w_ada, b_ada):
    tn = D_MODEL
    n_out = N_MOD * D_MODEL
    return pl.pallas_call(
        _mod_kernel,
        grid=(DEPTH, n_out // tn),
        in_specs=[
            pl.BlockSpec((C_ROWS, D_MODEL), lambda l, j: (0, 0)),
            pl.BlockSpec((1, D_MODEL, tn), lambda l, j: (l, 0, j)),
            pl.BlockSpec((1, 1, tn), lambda l, j: (l, 0, j)),
        ],
        out_specs=pl.BlockSpec((1, C_ROWS, tn), lambda l, j: (l, 0, j)),
        out_shape=jax.ShapeDtypeStruct((DEPTH, C_ROWS, n_out), F32),
        compiler_params=_params("parallel", "parallel"),
        name="adaln_mod",
    )(c_all, w_ada, b_ada.reshape(DEPTH, 1, n_out))


def _bias_kernel(rb_ref, bucket_ref, o_ref):
    bucket = bucket_ref[...]
    for h in range(Q_HEADS):
        acc = jnp.full(bucket.shape, -jnp.inf, F32)
        for b in range(NUM_BUCKETS):
            acc = jnp.where(bucket == b, rb_ref[b, h] * LOG2E, acc)
        g = h % GROUP
        o_ref[h // GROUP, :, g * Q_BLOCK:(g + 1) * Q_BLOCK] = acc


def _t5_bucket(rel):
    half = NUM_BUCKETS // 2
    max_exact = half // 2
    base = jnp.where(rel > 0, half, 0)
    n = jnp.abs(rel)
    nf = jnp.maximum(n, 1).astype(F32)
    large = max_exact + (jnp.log(nf / max_exact) / math.log(MAX_DISTANCE / max_exact)
                         * (half - max_exact)).astype(jnp.int32)
    large = jnp.minimum(large, half - 1)
    return (base + jnp.where(n < max_exact, n, large)).astype(jnp.int32)


def _band_bias(rel_bias):
    kk = jnp.arange(BAND)[:, None]
    qq = jnp.arange(Q_BLOCK)[None, :]
    rel = kk - WINDOW - qq
    bucket = jnp.where(jnp.abs(rel) <= WINDOW, _t5_bucket(rel), -1)
    return pl.pallas_call(
        _bias_kernel,
        in_specs=[
            pl.BlockSpec(memory_space=pltpu.SMEM),
            pl.BlockSpec((BAND, Q_BLOCK), lambda: (0, 0)),
        ],
        out_specs=pl.BlockSpec((KV_HEADS, BAND, GROUP * Q_BLOCK), lambda: (0, 0, 0)),
        out_shape=jax.ShapeDtypeStruct((KV_HEADS, BAND, GROUP * Q_BLOCK), F32),
        name="band_bias",
    )(rel_bias.astype(F32), bucket)


def _ada_norm(x, g, scale, shift):
    y = x * lax.rsqrt(jnp.mean(x * x, axis=-1, keepdims=True) + EPS)
    return y * (g * (1.0 + scale)) + shift


def _ffn_kernel(x_ref, mod_ref, g_ref, wi_ref, wo_ref, *rest, rows, mix_gate_row):
    sh_i, sc_i, gt_i = rows
    x = x_ref[0]
    if mix_gate_row is None:
        (o_ref,) = rest
    else:
        oa_ref, ob_ref, wmix_ref, o_ref = rest
        mix = jnp.concatenate([oa_ref[0], ob_ref[0]], axis=1)
        x = x + mod_ref[0, mix_gate_row:mix_gate_row + 1, :] * _dot(mix, wmix_ref[0])
    h = _ada_norm(x, g_ref[0], mod_ref[0, sc_i:sc_i + 1, :], mod_ref[0, sh_i:sh_i + 1, :]).astype(BF16)
    acc = jnp.zeros(x.shape, F32)
    for c in range(D_FF // MXU_DIM):
        lo = c * MXU_DIM
        a = _dot(h, wi_ref[0, :, lo:lo + MXU_DIM])
        b = _dot(h, wi_ref[0, :, D_FF + lo:D_FF + lo + MXU_DIM])
        acc = acc + _dot((_silu(a) * b).astype(BF16), wo_ref[0, lo:lo + MXU_DIM, :])
    o_ref[0] = x + 0.5 * mod_ref[0, gt_i:gt_i + 1, :] * acc


def _ffn(x, mod, g_norm, w_in, w_out, layer, sub, tm, mix=None):
    batch, seq, _ = x.shape
    rows = (3 * sub, 3 * sub + 1, 3 * sub + 2)
    resident = pl.Buffered(1)
    in_specs = [
        pl.BlockSpec((1, tm, D_MODEL), lambda b, i: (b, i, 0)),
        pl.BlockSpec((1, N_MOD, D_MODEL), lambda b, i: (b, 0, 0)),
        pl.BlockSpec((1, 1, D_MODEL), lambda b, i: (3 * layer + sub, 0, 0)),
        pl.BlockSpec((1, D_MODEL, 2 * D_FF), lambda b, i: (layer, 0, 0), pipeline_mode=resident),
        pl.BlockSpec((1, D_FF, D_MODEL), lambda b, i: (layer, 0, 0), pipeline_mode=resident),
    ]
    operands = [x, mod, g_norm, w_in, w_out]
    if mix is not None:
        in_specs += [
            pl.BlockSpec((1, tm, Q_W), lambda b, i: (b, i, 0)),
            pl.BlockSpec((1, tm, Q_W), lambda b, i: (b, i, 0)),
            pl.BlockSpec((1, 2 * Q_W, D_MODEL), lambda b, i: (layer, 0, 0), pipeline_mode=resident),
        ]
        operands += list(mix)
    return pl.pallas_call(
        functools.partial(_ffn_kernel, rows=rows, mix_gate_row=None if mix is None else 3 * sub - 1),
        grid=(batch, seq // tm),
        in_specs=in_specs,
        out_specs=pl.BlockSpec((1, tm, D_MODEL), lambda b, i: (b, i, 0)),
        out_shape=jax.ShapeDtypeStruct(x.shape, F32),
        compiler_params=_params("parallel", "parallel"),
        name="ffn",
    )(*operands)


def _head_sumsq(t, ones_bd):
    return _dot((t * t).astype(BF16), ones_bd)


def _rope(t, cos, sin, first_half):
    up = pltpu.roll(t, LANES - HEAD_DIM // 4, axis=1)
    dn = pltpu.roll(t, HEAD_DIM // 4, axis=1)
    return t * cos + jnp.where(first_half, up, dn) * sin


def _store_vx(vx_ref, v_t):
    ones = jnp.ones((VX_ROWS - HEAD_DIM, v_t.shape[1]), F32)
    for kv in range(KV_HEADS):
        vx_ref[0, kv] = jnp.concatenate(
            [v_t[kv * HEAD_DIM:(kv + 1) * HEAD_DIM], ones], axis=0).astype(BF16)


def _qkv_kernel(x_ref, mod_ref, g_ref, w_ref, gain_ref, cos_ref, sin_ref, ones_ref,
                qta_ref, ka_ref, vxa_ref, qtb_ref, kb_ref, vxb_ref):
    tm = x_ref.shape[1]
    h = _ada_norm(x_ref[0], g_ref[0], mod_ref[0, 4:5, :], mod_ref[0, 3:4, :]).astype(BF16)
    qkv = _dot(h, w_ref[0])
    ones_bd = ones_ref[...]
    gain = gain_ref[0]

    normed = []
    for s in range(QKV_W // MXU_DIM):
        t = qkv[:, s * MXU_DIM:(s + 1) * MXU_DIM]
        ss = _head_sumsq(t, ones_bd)
        normed.append(t * lax.rsqrt(ss * (1.0 / HEAD_DIM) + EPS)
                      * gain[:, s * MXU_DIM:(s + 1) * MXU_DIM])
    normed = jnp.concatenate(normed, axis=1)

    cos = cos_ref[...]
    sin = sin_ref[...]
    lane = lax.broadcasted_iota(jnp.int32, (tm, LANES), 1)
    first_half = (lane % (HEAD_DIM // 2)) < (HEAD_DIM // 4)

    qa = [_rope(normed[:, s * LANES:(s + 1) * LANES], cos, sin, first_half) * (SM_SCALE * LOG2E)
          for s in range(Q_W // LANES)]
    qa = jnp.concatenate(qa, axis=1)
    qta_ref[0] = qa.T.astype(BF16)
    ka_ref[0] = _rope(normed[:, Q_W:Q_W + KV_W], cos, sin, first_half).astype(BF16)
    _store_vx(vxa_ref, qkv[:, Q_W + KV_W:Q_W + 2 * KV_W].T)

    off = Q_W + 2 * KV_W
    qtb_ref[0] = (normed[:, off:off + Q_W] * (SM_SCALE * LOG2E)).T.astype(BF16)
    kb_ref[0] = normed[:, off + Q_W:off + Q_W + KV_W].astype(BF16)
    _store_vx(vxb_ref, qkv[:, off + Q_W + KV_W:off + Q_W + 2 * KV_W].T)


def _qkv(x, mod, g_norm, w_qkv, gains, cos, sin, ones_bd, layer, tm):
    batch, seq, _ = x.shape
    return pl.pallas_call(
        _qkv_kernel,
        grid=(batch, seq // tm),
        in_specs=[
            pl.BlockSpec((1, tm, D_MODEL), lambda b, i: (b, i, 0)),
            pl.BlockSpec((1, N_MOD, D_MODEL), lambda b, i: (b, 0, 0)),
            pl.BlockSpec((1, 1, D_MODEL), lambda b, i: (3 * layer + 1, 0, 0)),
            pl.BlockSpec((1, D_MODEL, QKV_W), lambda b, i: (layer, 0, 0)),
            pl.BlockSpec((1, 1, QKV_W), lambda b, i: (layer, 0, 0)),
            pl.BlockSpec((tm, LANES), lambda b, i: (i, 0)),
            pl.BlockSpec((tm, LANES), lambda b, i: (i, 0)),
            pl.BlockSpec((MXU_DIM, MXU_DIM), lambda b, i: (0, 0)),
        ],
        out_specs=2 * [
            pl.BlockSpec((1, Q_W, tm), lambda b, i: (b, 0, i)),
            pl.BlockSpec((1, tm, KV_W), lambda b, i: (b, i, 0)),
            pl.BlockSpec((1, KV_HEADS, VX_ROWS, tm), lambda b, i: (b, 0, 0, i)),
        ],
        out_shape=2 * [
            jax.ShapeDtypeStruct((batch, Q_W, seq), BF16),
            jax.ShapeDtypeStruct((batch, seq, KV_W), BF16),
            jax.ShapeDtypeStruct((batch, KV_HEADS, VX_ROWS, seq), BF16),
        ],
        compiler_params=_params("parallel", "parallel"),
        name="qkv",
    )(x, mod, g_norm, w_qkv, gains, cos, sin, ones_bd)


def _attn_a_kernel(qt_ref, k_ref, vx_ref, o_ref, s_scr, *, seq, tk, tiles_per_trip):
    tq = qt_ref.shape[2]
    nq = GROUP * tq
    gw = GROUP * HEAD_DIM
    n_tiles = seq // tk
    m_init = jnp.full((1, nq), -jnp.inf, F32)
    acc_init = jnp.zeros((VX_ROWS, nq), F32)

    def start(index, size):
        return index * size if isinstance(index, int) else pl.multiple_of(index * size, size)

    def padded_q(kv):
        q4 = qt_ref[0, pl.ds(start(kv, gw), gw), :]
        qt = jnp.concatenate([q4[g * HEAD_DIM:(g + 1) * HEAD_DIM] for g in range(GROUP)], axis=1)
        zero = jnp.zeros_like(qt)
        return jnp.concatenate([jnp.where(kv == h, qt, zero) for h in range(KV_HEADS)], axis=0)

    def scores(q_pad, j, slot, m):
        off = start(j, tk)
        s_t = _dot(k_ref[0, pl.ds(off, tk), :], q_pad)
        s_scr[slot] = s_t
        return jnp.maximum(m, jnp.max(s_t, axis=0, keepdims=True))

    def accumulate(kv, j, slot, m_before, m_now, acc):
        off = start(j, tk)
        alpha = jnp.exp2(m_before - m_now)
        p_t = jnp.exp2(s_scr[slot] - m_now).astype(BF16)
        return alpha * acc + _dot(vx_ref[0, kv, :, pl.ds(off, tk)], p_t)

    def store(kv, acc):
        out_t = acc[:HEAD_DIM] / acc[HEAD_DIM:HEAD_DIM + 1]
        out_t = jnp.concatenate([out_t[:, g * tq:(g + 1) * tq] for g in range(GROUP)], axis=0)
        o_ref[0, :, pl.ds(start(kv, gw), gw)] = out_t.T.astype(BF16)

    if n_tiles <= tiles_per_trip:
        chain = [(kv, j) for kv in range(KV_HEADS) for j in range(n_tiles)]
        q_pads = [padded_q(kv) for kv in range(KV_HEADS)]
        m_before = [m_init] * KV_HEADS
        m_now = [m_init] * KV_HEADS
        acc = [acc_init] * KV_HEADS
        m_ready = scores(q_pads[0], 0, 0, m_init)
        for c, (kv, j) in enumerate(chain):
            m_before[kv], m_now[kv] = m_now[kv], m_ready
            if c + 1 < len(chain):
                kv2, j2 = chain[c + 1]
                m_ready = scores(q_pads[kv2], j2, (c + 1) % 2, m_now[kv2])
            acc[kv] = accumulate(kv, j, c % 2, m_before[kv], m_now[kv], acc[kv])
        for kv in range(KV_HEADS):
            store(kv, acc[kv])
        return

    def sweep(kv, _):
        q_pad = padded_q(kv)

        def body(trip, carry):
            m_prev, m_cur, acc = carry
            j = tiles_per_trip * trip
            for u in range(tiles_per_trip):
                m_next = scores(q_pad, jnp.minimum(j + u + 1, n_tiles - 1), (u + 1) % 2, m_cur)
                acc = accumulate(kv, j + u, u % 2, m_prev, m_cur, acc)
                m_prev, m_cur = m_cur, m_next
            return m_prev, m_cur, acc

        m1 = scores(q_pad, 0, 0, m_init)
        _, _, acc = lax.fori_loop(0, n_tiles // tiles_per_trip, body, (m_init, m1, acc_init))
        store(kv, acc)
        return 0

    lax.fori_loop(0, KV_HEADS, sweep, 0)


def _attn_a(qt_a, k_a, vx_a, tq, tk, tiles_per_trip):
    batch, _, seq = qt_a.shape
    assert tiles_per_trip % 2 == 0 and (seq // tk) % tiles_per_trip == 0
    return pl.pallas_call(
        functools.partial(_attn_a_kernel, seq=seq, tk=tk, tiles_per_trip=tiles_per_trip),
        grid=(batch, seq // tq),
        in_specs=[
            pl.BlockSpec((1, Q_W, tq), lambda b, i: (b, 0, i)),
            pl.BlockSpec((1, seq, KV_W), lambda b, i: (b, 0, 0)),
            pl.BlockSpec((1, KV_HEADS, VX_ROWS, seq), lambda b, i: (b, 0, 0, 0)),
        ],
        out_specs=pl.BlockSpec((1, tq, Q_W), lambda b, i: (b, i, 0)),
        out_shape=jax.ShapeDtypeStruct((batch, seq, Q_W), BF16),
        scratch_shapes=[pltpu.VMEM((2, tk, GROUP * tq), F32)],
        compiler_params=_params("parallel", "parallel"),
        name="attn_global",
    )(qt_a, k_a, vx_a)


def _attn_b_kernel(sink_ref, qt_ref, kp_ref, kc_ref, kn_ref, vp_ref, vc_ref, vn_ref, bias_ref, o_ref, s_scr,
                   *, seq, layer, blocks):
    first_block = pl.program_id(1) * blocks
    n_blocks = seq // Q_BLOCK
    gw = GROUP * HEAD_DIM
    nq = GROUP * Q_BLOCK
    k_all = jnp.concatenate([kp_ref[0], kc_ref[0], kn_ref[0]], axis=0)
    lane_head = lax.broadcasted_iota(jnp.int32, (1, nq), 1) // Q_BLOCK
    zero = jnp.zeros((HEAD_DIM, nq), BF16)
    sinks = []
    for kv in range(KV_HEADS):
        sink = jnp.zeros((1, nq), F32)
        for g in range(GROUP):
            sink = jnp.where(lane_head == g, sink_ref[layer, kv * GROUP + g] * LOG2E, sink)
        sinks.append(sink)

    def scores(kv, blk, slot):
        lo = blk * Q_BLOCK
        q4 = qt_ref[0, kv * gw:(kv + 1) * gw, lo:lo + Q_BLOCK]
        qt = jnp.concatenate([q4[g * HEAD_DIM:(g + 1) * HEAD_DIM] for g in range(GROUP)], axis=1)
        q_pad = jnp.concatenate([qt, zero] if kv == 0 else [zero, qt], axis=0)
        s_t = _dot(k_all[lo:lo + BAND], q_pad) + bias_ref[kv]
        parts = [s_t[:Q_BLOCK], s_t[Q_BLOCK:2 * Q_BLOCK], s_t[2 * Q_BLOCK:]]
        if blk == 0:
            parts[0] = jnp.where(first_block > 0, parts[0], -jnp.inf)
        if blk == blocks - 1:
            parts[2] = jnp.where(first_block + blocks < n_blocks, parts[2], -jnp.inf)
        s_t = jnp.concatenate(parts, axis=0)
        s_scr[slot] = s_t
        return jnp.maximum(jnp.max(s_t, axis=0, keepdims=True), sinks[kv])

    def finish(kv, blk, slot, m):
        lo = blk * Q_BLOCK
        v_band = jnp.concatenate([vp_ref[0, kv], vc_ref[0, kv], vn_ref[0, kv]], axis=1)[:, lo:lo + BAND]
        p_t = jnp.exp2(s_scr[slot] - m).astype(BF16)
        acc = _dot(v_band, p_t)
        denom = acc[HEAD_DIM:HEAD_DIM + 1] + jnp.exp2(sinks[kv] - m)
        out_t = acc[:HEAD_DIM] / denom
        out_t = jnp.concatenate([out_t[:, g * Q_BLOCK:(g + 1) * Q_BLOCK] for g in range(GROUP)], axis=0)
        o_ref[0, lo:lo + Q_BLOCK, kv * gw:(kv + 1) * gw] = out_t.T.astype(BF16)

    chains = [(kv, blk) for kv in range(KV_HEADS) for blk in range(blocks)]
    m_next = scores(*chains[0], 0)
    for c, chain in enumerate(chains):
        m = m_next
        if c + 1 < len(chains):
            m_next = scores(*chains[c + 1], (c + 1) % 2)
        finish(*chain, c % 2, m)


def _attn_b(qt_b, k_b, vx_b, band_bias, sinks, layer, blocks):
    batch, _, seq = qt_b.shape
    nb = seq // Q_BLOCK
    tq = blocks * Q_BLOCK
    prev_blk = lambda i: jnp.maximum(i * blocks - 1, 0)
    next_blk = lambda i: jnp.minimum((i + 1) * blocks, nb - 1)
    return pl.pallas_call(
        functools.partial(_attn_b_kernel, seq=seq, layer=layer, blocks=blocks),
        grid=(batch, seq // tq),
        in_specs=[
            pl.BlockSpec(memory_space=pltpu.SMEM),
            pl.BlockSpec((1, Q_W, tq), lambda b, i: (b, 0, i)),
            pl.BlockSpec((1, Q_BLOCK, KV_W), lambda b, i: (b, prev_blk(i), 0)),
            pl.BlockSpec((1, tq, KV_W), lambda b, i: (b, i, 0)),
            pl.BlockSpec((1, Q_BLOCK, KV_W), lambda b, i: (b, next_blk(i), 0)),
            pl.BlockSpec((1, KV_HEADS, VX_ROWS, Q_BLOCK), lambda b, i: (b, 0, 0, prev_blk(i))),
            pl.BlockSpec((1, KV_HEADS, VX_ROWS, tq), lambda b, i: (b, 0, 0, i)),
            pl.BlockSpec((1, KV_HEADS, VX_ROWS, Q_BLOCK), lambda b, i: (b, 0, 0, next_blk(i))),
            pl.BlockSpec((KV_HEADS, BAND, GROUP * Q_BLOCK), lambda b, i: (0, 0, 0)),
        ],
        out_specs=pl.BlockSpec((1, tq, Q_W), lambda b, i: (b, i, 0)),
        out_shape=jax.ShapeDtypeStruct((batch, seq, Q_W), BF16),
        scratch_shapes=[pltpu.VMEM((2, BAND, GROUP * Q_BLOCK), F32)],
        compiler_params=_params("parallel", "parallel"),
        name="attn_window",
    )(sinks, qt_b, k_b, k_b, k_b, vx_b, vx_b, vx_b, band_bias)


def _rope_tables(seq):
    pos = jnp.arange(seq)
    row = (pos // GRID_W).astype(F32)
    col = (pos % GRID_W).astype(F32)
    axis_dim = HEAD_DIM // 2
    freqs = ROPE_THETA ** (-jnp.arange(0, axis_dim, 2, dtype=F32) / axis_dim)
    ang_r = row[:, None] * freqs[None, :]
    ang_c = col[:, None] * freqs[None, :]
    cr, sr, cc, sc = jnp.cos(ang_r), jnp.sin(ang_r), jnp.cos(ang_c), jnp.sin(ang_c)
    cos = jnp.concatenate([cr, cr, cc, cc], axis=1)
    sin = jnp.concatenate([-sr, sr, -sc, sc], axis=1)
    reps = LANES // HEAD_DIM
    return jnp.tile(cos, (1, reps)), jnp.tile(sin, (1, reps))


def _qkv_gains(g_qk):
    ones = jnp.ones((DEPTH, KV_W), F32)
    t = lambda i, n: jnp.tile(g_qk[:, i, :].astype(F32), (1, n))
    return jnp.concatenate([t(0, Q_HEADS), t(1, KV_HEADS), ones,
                            t(2, Q_HEADS), t(3, KV_HEADS), ones], axis=1)[:, None, :]


def _tiles(seq):
    tm_ffn = min(512, seq)
    tm_qkv = min(512, seq)
    tq = min(2 * Q_BLOCK, seq)
    tk = min(MXU_DIM, seq)
    tiles_per_trip = min(32, seq // tk)
    win_blocks = min(16, seq // Q_BLOCK)
    return tm_ffn, tm_qkv, tq, tk, tiles_per_trip, win_blocks


def _trunk(x, mod_all, weights, band_bias):
    seq = x.shape[1]
    tm_ffn, tm_qkv, tq, tk, tiles_per_trip, win_blocks = _tiles(seq)
    cos, sin = _rope_tables(seq)
    for l in range(DEPTH):
        mod = mod_all[l]
        x = _ffn(x, mod, weights["g_norm"], weights["ffn1_w_in"], weights["ffn1_w_out"], l, 0, tm_ffn)
        qt_a, k_a, vx_a, qt_b, k_b, vx_b = _qkv(x, mod, weights["g_norm"], weights["w_qkv"], weights["gains"],
                                                cos, sin, weights["ones_bd"], l, tm_qkv)
        out_a = _attn_a(qt_a, k_a, vx_a, tq, tk, tiles_per_trip)
        out_b = _attn_b(qt_b, k_b, vx_b, band_bias, weights["sinks"], l, win_blocks)
        x = _ffn(x, mod, weights["g_norm"], weights["ffn2_w_in"], weights["ffn2_w_out"], l, 2, tm_ffn,
                 mix=(out_a, out_b, weights["w_o"]))
    return x


def _prepare(g_norm, ffn1_w_in, ffn1_w_out, w_qkv, g_qk, attn_sinks, w_o, ffn2_w_in, ffn2_w_out):
    head_id = np.arange(MXU_DIM) // HEAD_DIM
    return {
        "g_norm": g_norm.astype(F32).reshape(DEPTH * 3, 1, D_MODEL),
        "ffn1_w_in": ffn1_w_in.astype(BF16), "ffn1_w_out": ffn1_w_out.astype(BF16),
        "ffn2_w_in": ffn2_w_in.astype(BF16), "ffn2_w_out": ffn2_w_out.astype(BF16),
        "w_qkv": w_qkv.astype(BF16), "w_o": w_o.astype(BF16),
        "gains": _qkv_gains(g_qk),
        "sinks": attn_sinks.astype(F32),
        "ones_bd": jnp.asarray(head_id[:, None] == head_id[None, :], BF16),
    }


def kernel(x_prompt, x_sample, c_prompt, c_sample, w_ada, b_ada, g_norm, ffn1_w_in, ffn1_w_out,
           w_qkv, g_qk, attn_sinks, rel_bias, w_o, ffn2_w_in, ffn2_w_out):
    weights = _prepare(g_norm, ffn1_w_in, ffn1_w_out, w_qkv, g_qk, attn_sinks, w_o, ffn2_w_in, ffn2_w_out)
    nb_p, nb_s = c_prompt.shape[0], c_sample.shape[0]
    c_all = jnp.concatenate(
        [c_prompt, c_sample, jnp.zeros((C_ROWS - nb_p - nb_s, D_MODEL), F32)], axis=0)
    mod = _modulation(c_all, w_ada, b_ada).reshape(DEPTH, C_ROWS, N_MOD, D_MODEL)
    band_bias = _band_bias(rel_bias)
    y_prompt = _trunk(x_prompt, mod[:, :nb_p], weights, band_bias)
    y_sample = _trunk(x_sample, mod[:, nb_p:nb_p + nb_s], weights, band_bias)
    return (y_prompt, y_sample)
```
